```python
import jax, jax.numpy as jnp
from jax import lax
import numpy as np

D_MODEL = 2048
BATCH = 16
SEQ = 256
DEPTH = 2
DEC_BATCH = 2
DEC_SEQ = 2048
PAST_LEN = 256

GRID_W = 64
EPS = 1e-6
HEAD_DIM = 128
ATT_Q_HEADS = 8
ATT_KV_HEADS = 2
ATT_GROUP = ATT_Q_HEADS // ATT_KV_HEADS
ATT_WIDTH = ATT_Q_HEADS * HEAD_DIM
KV_WIDTH = ATT_KV_HEADS * HEAD_DIM
Q_BLOCK = 128
ROPE_THETA = 10000.0
ROPE_FREQS = HEAD_DIM // 4
M_HEADS = 4
M_DK = 256
M_DV = 256
M_WIDTH = M_HEADS * M_DV
M_CHUNK = 128
CONV_WIDTH = 1024
CONV_K = 3
N_BRANCH = 3
BRANCH_W = 1024
FF_HIDDEN = ((8 * D_MODEL + 767) // 768) * 256
IN_WIDTH = ATT_WIDTH + 2 * KV_WIDTH + 2 * M_HEADS * M_DK + 2 * M_WIDTH + 4 * M_HEADS + 3 * CONV_WIDTH + N_BRANCH * D_MODEL

kernel_name = 'hybrid_gqa_mlstm_conv_diffusion_step'


def rmsnorm(x, g):
    xf = x.astype(jnp.float32)
    y = xf * lax.rsqrt(jnp.mean(xf * xf, axis=-1, keepdims=True) + EPS)
    return (y * g.astype(jnp.float32)).astype(x.dtype)


def axial_rope(n_tokens):
    rows = n_tokens // GRID_W
    row = jnp.repeat(jnp.arange(rows), GRID_W)
    col = jnp.tile(jnp.arange(GRID_W), rows)
    inv = ROPE_THETA ** (-jnp.arange(ROPE_FREQS, dtype=jnp.float32) / ROPE_FREQS)
    ang = jnp.stack([row, col], axis=-1).astype(jnp.float32)[:, :, None] * inv
    return jnp.cos(ang), jnp.sin(ang)


def apply_rope(x, cos, sin):
    B, T, H, _ = x.shape
    xr = x.reshape(B, T, H, 2, 2, ROPE_FREQS).astype(jnp.float32)
    x1, x2 = xr[..., 0, :], xr[..., 1, :]
    c = cos[None, :, None]
    s = sin[None, :, None]
    out = jnp.stack([x1 * c - x2 * s, x2 * c + x1 * s], axis=-2)
    return out.reshape(x.shape).astype(x.dtype)


def attend(q, k, v):
    B, T = q.shape[0], q.shape[1]
    nb = T // Q_BLOCK
    qb = (q * HEAD_DIM ** -0.5).reshape(B, nb, Q_BLOCK, ATT_KV_HEADS, ATT_GROUP, HEAD_DIM)
    qb = jnp.moveaxis(qb, 1, 0)

    def block(qblk):
        s = jnp.einsum('bqkgd,bskd->bkgqs', qblk, k).astype(jnp.float32)
        p = jax.nn.softmax(s, axis=-1).astype(v.dtype)
        return jnp.einsum('bkgqs,bskd->bqkgd', p, v)

    o = lax.map(block, qb)
    return jnp.moveaxis(o, 0, 1).reshape(B, T, ATT_WIDTH)


def mlstm_scan(q, k, v, ig, lf, C0, n0, m0):
    B, T, H, _ = q.shape
    nc = T // M_CHUNK

    def chunks(a):
        a = a.reshape((B, nc, M_CHUNK, H) + a.shape[3:])
        return jnp.moveaxis(a, (1, 3), (0, 2))

    tril = jnp.tril(jnp.ones((M_CHUNK, M_CHUNK), dtype=bool))

    def step(carry, xs):
        C, n, m = carry
        qc, kc, vc, ic, fc = xs
        b = jnp.cumsum(fc, axis=-1)
        logd = jnp.where(tril, b[..., :, None] - b[..., None, :] + ic[..., None, :], -jnp.inf)
        g = b + m[..., None]
        mt = jnp.maximum(g, jnp.max(logd, axis=-1))
        s = jnp.einsum('bhtd,bhsd->bhts', qc, kc) * jnp.exp(logd - mt[..., None])
        inter = jnp.exp(g - mt)
        num = inter[..., None] * jnp.einsum('bhtd,bhdv->bhtv', qc, C) + jnp.einsum('bhts,bhsv->bhtv', s, vc)
        den = inter * jnp.einsum('bhtd,bhd->bht', qc, n) + jnp.sum(s, axis=-1)
        h = num / jnp.maximum(jnp.abs(den), jnp.exp(-mt))[..., None]
        bl = b[..., -1]
        wl = bl[..., None] - b + ic
        mn = jnp.maximum(bl + m, jnp.max(wl, axis=-1))
        w = jnp.exp(wl - mn[..., None])
        dec = jnp.exp(bl + m - mn)
        Cn = dec[..., None, None] * C + jnp.einsum('bhs,bhsd,bhsv->bhdv', w, kc, vc)
        nn_ = dec[..., None] * n + jnp.einsum('bhs,bhsd->bhd', w, kc)
        return (Cn, nn_, mn), h

    (C, n, m), h = lax.scan(step, (C0, n0, m0), (chunks(q), chunks(k), chunks(v), chunks(ig), chunks(lf)))
    h = jnp.moveaxis(h, (0, 2), (1, 3)).reshape(B, T, H, v.shape[-1])
    return h, (C, n, m)


def conv3(u, w):
    up = jnp.pad(u, ((0, 0), (1, 1), (0, 0)))
    return w[0] * up[:, :-2] + w[1] * up[:, 1:-1] + w[2] * up[:, 2:]


def modulation(cond, w_mod, b_mod):
    mod = jax.nn.silu(cond) @ w_mod + b_mod
    return jnp.split(mod[:, None, :], 6, axis=-1)


def swiglu(h, w_ffn_in, w_ffn_out):
    gate, up = jnp.split(h @ w_ffn_in, 2, axis=-1)
    return (jax.nn.silu(gate) * up) @ w_ffn_out


def mixer(h, ctx_kv, init_f, init_b, rope, w_in, q_norm, k_norm, gate_bias, m_norm, conv_w, w_branch, w_out):
    B, T, _ = h.shape
    sizes = (ATT_WIDTH, KV_WIDTH, KV_WIDTH, M_HEADS * M_DK, M_HEADS * M_DK, M_WIDTH, M_WIDTH,
             4 * M_HEADS, CONV_WIDTH, CONV_WIDTH, CONV_WIDTH, N_BRANCH * D_MODEL)
    (aq, ak, av, mq, mk, mv, mo, mg, cb, cc, cx, gl) = jnp.split(
        h @ w_in, np.cumsum(sizes)[:-1].tolist(), axis=-1)

    q = rmsnorm(aq.reshape(B, T, ATT_Q_HEADS, HEAD_DIM), q_norm)
    k = rmsnorm(ak.reshape(B, T, ATT_KV_HEADS, HEAD_DIM), k_norm)
    v = av.reshape(B, T, ATT_KV_HEADS, HEAD_DIM)
    if ctx_kv is None:
        att = attend(q, k, v)
    else:
        cos, sin = rope
        keys = jnp.concatenate([ctx_kv[0].astype(k.dtype), apply_rope(k, cos, sin)], axis=1)
        vals = jnp.concatenate([ctx_kv[1].astype(v.dtype), v], axis=1)
        att = attend(apply_rope(q, cos, sin), keys, vals)

    f32 = jnp.float32
    gates = (mg.reshape(B, T, 4, M_HEADS) + gate_bias.reshape(4, M_HEADS)).astype(f32)
    mqh = mq.reshape(B, T, M_HEADS, M_DK).astype(f32)
    mkh = mk.reshape(B, T, M_HEADS, M_DK).astype(f32) * M_DK ** -0.5
    mvh = mv.reshape(B, T, M_HEADS, M_DV).astype(f32)
    hf, sf = mlstm_scan(mqh, mkh, mvh, gates[:, :, 0], jax.nn.log_sigmoid(gates[:, :, 1]), *init_f)
    rev = lambda a: jnp.flip(a, axis=1)
    hb, sb = mlstm_scan(rev(mqh), rev(mkh), rev(mvh), rev(gates[:, :, 2]),
                        rev(jax.nn.log_sigmoid(gates[:, :, 3])), *init_b)
    hm = rmsnorm(hf + rev(hb), m_norm.reshape(M_HEADS, M_DV)).reshape(B, T, M_WIDTH).astype(h.dtype)
    mlstm_out = hm * jax.nn.sigmoid(mo)

    conv_out = cb * conv3(cc * cx, conv_w)

    branches = jnp.stack([att, mlstm_out, conv_out], axis=2)
    up = jnp.einsum('btgc,gcd->btgd', branches, w_branch)
    merged = jnp.sum(jax.nn.sigmoid(gl.reshape(B, T, N_BRANCH, D_MODEL)) * up, axis=2)
    return merged @ w_out, k, v, sf, sb


def layer(x, cond, ctx_kv, init_f, init_b, rope, w_mod, b_mod, n_pre1, n_post1, n_pre2, n_post2,
          w_in, q_norm, k_norm, gate_bias, m_norm, conv_w, w_branch, w_out, w_ffn_in, w_ffn_out):
    sh1, sc1, g1, sh2, sc2, g2 = modulation(cond, w_mod, b_mod)
    h = rmsnorm(x, n_pre1) * (1 + sc1) + sh1
    mix, k, v, sf, sb = mixer(h, ctx_kv, init_f, init_b, rope, w_in, q_norm, k_norm, gate_bias,
                              m_norm, conv_w, w_branch, w_out)
    x = x + g1 * rmsnorm(mix, n_post1)
    h = rmsnorm(x, n_pre2) * (1 + sc2) + sh2
    x = x + g2 * rmsnorm(swiglu(h, w_ffn_in, w_ffn_out), n_post2)
    return x, k, v, sf, sb


def setup_inputs(seed: int = 0) -> dict:
    key = jax.random.key(seed)
    ks = jax.random.split(key, 32)
    f32 = jnp.float32
    nrm = lambda k, shape, s: jax.random.normal(k, shape, f32) * s
    D = D_MODEL
    ib = nrm(ks[20], (DEPTH, 2, M_HEADS), 0.1)
    fb = 3.0 + 3.0 * jax.random.uniform(ks[21], (DEPTH, 2, M_HEADS), f32)
    gate_bias = jnp.stack([ib[:, 0], fb[:, 0], ib[:, 1], fb[:, 1]], axis=1).reshape(DEPTH, 4 * M_HEADS)
    return {
        'x_prompt': nrm(ks[0], (BATCH, SEQ, D), 1.0),
        'x_sample': nrm(ks[1], (DEC_BATCH, DEC_SEQ, D), 1.0),
        'cache_k': nrm(ks[2], (DEC_BATCH, DEPTH, PAST_LEN, ATT_KV_HEADS, HEAD_DIM), 1.0),
        'cache_v': nrm(ks[3], (DEC_BATCH, DEPTH, PAST_LEN, ATT_KV_HEADS, HEAD_DIM), 1.0),
        'state_C': nrm(ks[4], (DEC_BATCH, DEPTH, 2, M_HEADS, M_DK, M_DV), 0.05),
        'state_n': nrm(ks[5], (DEC_BATCH, DEPTH, 2, M_HEADS, M_DK), 0.1),
        'state_m': nrm(ks[6], (DEC_BATCH, DEPTH, 2, M_HEADS), 1.0),
        'c': nrm(ks[7], (DEC_BATCH, D), 1.0),
        'c_ctx': nrm(ks[8], (D,), 1.0),
        'w_mod': nrm(ks[9], (DEPTH, D, 6 * D), 0.5 * D ** -0.5),
        'b_mod': nrm(ks[10], (DEPTH, 6 * D), 0.01),
        'norm_pre1': 1.0 + nrm(ks[11], (DEPTH, D), 0.1),
        'norm_post1': 1.0 + nrm(ks[12], (DEPTH, D), 0.1),
        'norm_pre2': 1.0 + nrm(ks[13], (DEPTH, D), 0.1),
        'norm_post2': 1.0 + nrm(ks[14], (DEPTH, D), 0.1),
        'w_in': nrm(ks[15], (DEPTH, D, IN_WIDTH), D ** -0.5),
        'q_norm': 1.0 + nrm(ks[16], (DEPTH, HEAD_DIM), 0.1),
        'k_norm': 1.0 + nrm(ks[17], (DEPTH, HEAD_DIM), 0.1),
        'mlstm_gate_bias': gate_bias,
        'mlstm_norm': 1.0 + nrm(ks[18], (DEPTH, M_WIDTH), 0.1),
        'conv_w': nrm(ks[19], (DEPTH, CONV_K, CONV_WIDTH), CONV_K ** -0.5),
        'w_branch': nrm(ks[22], (DEPTH, N_BRANCH, BRANCH_W, D), BRANCH_W ** -0.5),
        'w_out': nrm(ks[23], (DEPTH, D, D), D ** -0.5),
        'w_ffn_in': nrm(ks[24], (DEPTH, D, 2 * FF_HIDDEN), D ** -0.5),
        'w_ffn_out': nrm(ks[25], (DEPTH, FF_HIDDEN, D), FF_HIDDEN ** -0.5),
    }


def reference(x_prompt, x_sample, cache_k, cache_v, state_C, state_n, state_m, c, c_ctx,
              w_mod, b_mod, norm_pre1, norm_post1, norm_pre2, norm_post2, w_in, q_norm, k_norm,
              mlstm_gate_bias, mlstm_norm, conv_w, w_branch, w_out, w_ffn_in, w_ffn_out):
    f32 = jnp.float32
    nb = x_prompt.shape[0]
    zero = (jnp.zeros((nb, M_HEADS, M_DK, M_DV), f32), jnp.zeros((nb, M_HEADS, M_DK), f32),
            jnp.zeros((nb, M_HEADS), f32))
    rope = axial_rope(x_sample.shape[1])
    xp, xs = x_prompt, x_sample
    ks_, vs_, Cs, ns, ms = [], [], [], [], []
    for l in range(DEPTH):
        lw = (w_mod[l], b_mod[l], norm_pre1[l], norm_post1[l], norm_pre2[l], norm_post2[l],
              w_in[l], q_norm[l], k_norm[l], mlstm_gate_bias[l], mlstm_norm[l], conv_w[l],
              w_branch[l], w_out[l], w_ffn_in[l], w_ffn_out[l])
        xp, k, v, sf, sb = layer(xp, c_ctx[None], None, zero, zero, None, *lw)
        ks_.append(k)
        vs_.append(v)
        Cs.append(jnp.stack([sf[0], sb[0]], axis=1))
        ns.append(jnp.stack([sf[1], sb[1]], axis=1))
        ms.append(jnp.stack([sf[2], sb[2]], axis=1))
        init_f = (state_C[:, l, 0].astype(f32), state_n[:, l, 0].astype(f32), state_m[:, l, 0].astype(f32))
        init_b = (state_C[:, l, 1].astype(f32), state_n[:, l, 1].astype(f32), state_m[:, l, 1].astype(f32))
        xs, _, _, _, _ = layer(xs, c, (cache_k[:, l], cache_v[:, l]), init_f, init_b, rope, *lw)
    new_cache_k = jnp.stack(ks_, axis=1)
    new_cache_v = jnp.stack(vs_, axis=1)
    new_state_C = jnp.stack(Cs, axis=1)
    new_state_n = jnp.stack(ns, axis=1)
    new_state_m = jnp.stack(ms, axis=1)
    return (xp, xs, new_cache_k, new_cache_v, new_state_C, new_state_n, new_state_m)
```

```python
import functools

import jax
import jax.numpy as jnp
import numpy as np
from jax import lax
from jax.experimental import pallas as pl
from jax.experimental.pallas import tpu as pltpu

F32 = jnp.float32
BF16 = jnp.bfloat16

D_MODEL = 2048
BATCH = 16
SEQ = 256
DEPTH = 2
DEC_BATCH = 2
DEC_SEQ = 2048
PAST_LEN = 256
GRID_W = 64
EPS = 1e-6
HEAD_DIM = 128
ATT_Q_HEADS = 8
ATT_KV_HEADS = 2
ATT_GROUP = ATT_Q_HEADS // ATT_KV_HEADS
ATT_WIDTH = ATT_Q_HEADS * HEAD_DIM
KV_WIDTH = ATT_KV_HEADS * HEAD_DIM
ROPE_THETA = 10000.0
ROPE_FREQS = HEAD_DIM // 4
M_HEADS = 4
M_DK = 256
M_DV = 256
M_WIDTH = M_HEADS * M_DV
M_CHUNK = 128
CONV_WIDTH = 1024
N_BRANCH = 3
BRANCH_W = 1024
FF_HIDDEN = 5632

N_P = BATCH * SEQ
N_S = DEC_BATCH * DEC_SEQ
N_TOK = N_P + N_S
N_COND = 8
KV_LEN_S = PAST_LEN + DEC_SEQ

COL_MQ, COL_MK, COL_MV, COL_MO = 0, 1024, 2048, 3072
COL_AQ, COL_CB, COL_CC, COL_CX = 4096, 5120, 6144, 7168
COL_GL = 8192
COL_AK = COL_GL + N_BRANCH * D_MODEL
COL_AV = COL_AK + KV_WIDTH
PROJ_W = COL_AV + KV_WIDTH
GATE_PAD = 128

V7X_VMEM_BYTES = 64 * 1024 * 1024


def _cparams(semantics, vmem_mb):
    return pltpu.CompilerParams(dimension_semantics=semantics,
                                vmem_limit_bytes=vmem_mb * 1024 * 1024)


def _rms(x):
    return x * lax.rsqrt(jnp.mean(x * x, axis=-1, keepdims=True) + EPS)


def _cond_row(tile, tm):
    return jnp.where(tile < N_P // tm, 0, 1 + (tile * tm - N_P) // DEC_SEQ)


def _mod_kernel(c_ref, w_ref, b_ref, o_ref):
    c = c_ref[...]
    a = (c * jax.nn.sigmoid(c)).astype(BF16)
    o_ref[...] = jnp.dot(a, w_ref[...].astype(BF16), preferred_element_type=F32) + b_ref[...]


def _modulation(cond, w_mod, b_mod):
    tn = 1024
    n = 6 * D_MODEL
    return pl.pallas_call(
        _mod_kernel,
        grid=(DEPTH, n // tn),
        in_specs=[
            pl.BlockSpec((N_COND, D_MODEL), lambda l, j: (0, 0)),
            pl.BlockSpec((None, D_MODEL, tn), lambda l, j: (l, 0, j)),
            pl.BlockSpec((None, 1, tn), lambda l, j: (l, 0, j)),
        ],
        out_specs=pl.BlockSpec((None, N_COND, tn), lambda l, j: (l, 0, j)),
        out_shape=jax.ShapeDtypeStruct((DEPTH, N_COND, n), F32),
        compiler_params=_cparams(("parallel", "parallel"), 32),
        name="modulation",
    )(cond, w_mod, b_mod.reshape(DEPTH, 1, n))


def _prenorm_kernel(x_ref, g_ref, sc_ref, sh_ref, o_ref):
    y = _rms(x_ref[...]) * g_ref[...]
    o_ref[...] = (y * (1 + sc_ref[...]) + sh_ref[...]).astype(BF16)


def _prenorm(x, g, sc, sh):
    tm = 512
    row = lambda i: (_cond_row(i, tm), 0, 0)
    return pl.pallas_call(
        _prenorm_kernel,
        grid=(N_TOK // tm,),
        in_specs=[
            pl.BlockSpec((tm, D_MODEL), lambda i: (i, 0)),
            pl.BlockSpec((1, D_MODEL), lambda i: (0, 0)),
            pl.BlockSpec((None, 1, D_MODEL), row),
            pl.BlockSpec((None, 1, D_MODEL), row),
        ],
        out_specs=pl.BlockSpec((tm, D_MODEL), lambda i: (i, 0)),
        out_shape=jax.ShapeDtypeStruct((N_TOK, D_MODEL), BF16),
        compiler_params=_cparams(("parallel",), 32),
        name="prenorm",
    )(x, g, sc, sh)


def _matmul_kernel(x_ref, w_ref, o_ref):
    o_ref[...] = jnp.dot(x_ref[...], w_ref[...], preferred_element_type=F32)


def _projection(h, w):
    tm, tn = 1024, 512
    k = h.shape[1]
    n = w.shape[1]
    return pl.pallas_call(
        _matmul_kernel,
        grid=(N_TOK // tm, n // tn),
        in_specs=[
            pl.BlockSpec((tm, k), lambda i, j: (i, 0)),
            pl.BlockSpec((k, tn), lambda i, j: (0, j)),
        ],
        out_specs=pl.BlockSpec((tm, tn), lambda i, j: (i, j)),
        out_shape=jax.ShapeDtypeStruct((N_TOK, n), F32),
        compiler_params=_cparams(("parallel", "arbitrary"), 40),
        name="projection",
    )(h, w)


def _gate_kernel(h_ref, w_ref, b_ref, col_ref, row_ref, *, chunks):
    g = jnp.dot(h_ref[...], w_ref[...], preferred_element_type=F32) + b_ref[...]
    lf = jax.nn.log_sigmoid(g)
    r = lax.broadcasted_iota(jnp.int32, (M_CHUNK, M_CHUNK), 0)
    c = lax.broadcasted_iota(jnp.int32, (M_CHUNK, M_CHUNK), 1)
    tril = (c <= r).astype(F32)
    triu = (c >= r).astype(F32)
    lane = lax.broadcasted_iota(jnp.int32, (M_CHUNK, GATE_PAD), 1)
    for ch in range(chunks):
        rows = slice(ch * M_CHUNK, (ch + 1) * M_CHUNK)
        gc = g[rows]
        lfc = lf[rows]
        pre = jnp.dot(tril, lfc, preferred_element_type=F32, precision=lax.Precision.HIGHEST)
        suf = jnp.dot(triu, lfc, preferred_element_type=F32, precision=lax.Precision.HIGHEST)
        both = jnp.where((lane >= 4) & (lane < 8), pre,
                         jnp.where((lane >= 12) & (lane < 16), suf, gc))
        bwd = pltpu.roll(both, GATE_PAD - 8, axis=1)
        col_ref[0, rows, :] = both
        col_ref[1, rows, :] = bwd
        row_ref[0, ch] = both.T[0:8, :]
        row_ref[1, ch] = bwd.T[0:8, :]


def _gates(h, wg, bias):
    tm = 512
    chunks = tm // M_CHUNK
    return pl.pallas_call(
        functools.partial(_gate_kernel, chunks=chunks),
        grid=(N_TOK // tm,),
        in_specs=[
            pl.BlockSpec((tm, D_MODEL), lambda i: (i, 0)),
            pl.BlockSpec((D_MODEL, GATE_PAD), lambda i: (0, 0)),
            pl.BlockSpec((1, GATE_PAD), lambda i: (0, 0)),
        ],
        out_specs=[
            pl.BlockSpec((2, tm, GATE_PAD), lambda i: (0, i, 0)),
            pl.BlockSpec((2, chunks, 8, M_CHUNK), lambda i: (0, i, 0, 0)),
        ],
        out_shape=[
            jax.ShapeDtypeStruct((2, N_TOK, GATE_PAD), F32),
            jax.ShapeDtypeStruct((2, N_TOK // M_CHUNK, 8, M_CHUNK), F32),
        ],
        compiler_params=_cparams(("parallel",), 32),
        name="mlstm_gates",
    )(h, wg, bias)


def _mlstm_kernel(*refs, has_init, emit_state):
    q_ref, k_ref, v_ref, col_ref, row_ref = refs[:5]
    pos = 5
    if has_init:
        c0_ref, n0_ref, m0_ref = refs[pos:pos + 3]
        pos += 3
    h_ref = refs[pos]
    pos += 1
    if emit_state:
        cout_ref, nout_ref, mout_ref = refs[pos:pos + 3]
        pos += 3
    c_scr, n_scr, m_scr = refs[pos:pos + 3]

    d = pl.program_id(1)
    ci = pl.program_id(2)
    last = pl.num_programs(2) - 1

    @pl.when(ci == 0)
    def _():
        if has_init:
            c_scr[...] = c0_ref[...]
            n_scr[...] = n0_ref[...]
            m_scr[...] = m0_ref[...]
        else:
            c_scr[...] = jnp.zeros_like(c_scr)
            n_scr[...] = jnp.zeros_like(n_scr)
            m_scr[...] = jnp.zeros_like(m_scr)

    col = col_ref[...]
    row = row_ref[...]
    t_idx = lax.broadcasted_iota(jnp.int32, (M_CHUNK, M_CHUNK), 0)
    s_idx = lax.broadcasted_iota(jnp.int32, (M_CHUNK, M_CHUNK), 1)
    mask = (s_idx - t_idx) * (1 - 2 * d) <= 0
    fwd = d == 0

    for hd in range(M_HEADS):
        cols = slice(hd * M_DK, (hd + 1) * M_DK)
        q32 = q_ref[:, cols]
        k32 = k_ref[:, cols] * (M_DK ** -0.5)
        v32 = v_ref[:, cols]
        q = q32.astype(BF16)
        k = k32.astype(BF16)
        v = v32.astype(BF16)
        i_col = col[:, hd:hd + 1]
        b_col = col[:, 4 + hd:5 + hd]
        i_row = row[hd:hd + 1, :]
        b_row = row[4 + hd:5 + hd, :]
        bl = jnp.where(fwd, b_row[:, M_CHUNK - 1:M_CHUNK], b_row[:, 0:1])
        c_prev = c_scr[hd]
        n_prev = n_scr[hd]
        m_prev = m_scr[hd][:, 0:1]

        logd = jnp.where(mask, b_col - b_row + i_row, -jnp.inf)
        g = b_col + m_prev
        mt = jnp.maximum(g, jnp.max(logd, axis=-1, keepdims=True))
        qk = lax.dot_general(q, k, (((1,), (1,)), ((), ())), preferred_element_type=F32)
        s = qk * jnp.exp(logd - mt)
        inter = jnp.exp(g - mt)
        num = inter * jnp.dot(q, c_prev.astype(BF16), preferred_element_type=F32) \
            + jnp.dot(s.astype(BF16), v, preferred_element_type=F32)
        den = inter * jnp.sum(q32 * n_prev, axis=-1, keepdims=True) \
            + jnp.sum(s, axis=-1, keepdims=True)
        h_ref[:, cols] = num / jnp.maximum(jnp.abs(den), jnp.exp(-mt))

        wl_row = bl - b_row + i_row
        wl_col = bl - b_col + i_col
        m_new = jnp.maximum(bl + m_prev, jnp.max(wl_row, axis=-1, keepdims=True))
        w_col = jnp.exp(wl_col - m_new)
        dec = jnp.exp(bl + m_prev - m_new)
        wv = (w_col * v32).astype(BF16)
        c_new = dec * c_prev + lax.dot_general(k, wv, (((0,), (0,)), ((), ())),
                                               preferred_element_type=F32)
        n_new = dec * n_prev + jnp.sum(w_col * k32, axis=0, keepdims=True)
        c_scr[hd] = c_new
        n_scr[hd] = n_new
        m_scr[hd] = jnp.broadcast_to(m_new, (1, 128))

    if emit_state:
        @pl.when(ci == last)
        def _():
            cout_ref[...] = c_scr[...]
            nout_ref[...] = n_scr[...]
            mout_ref[...] = m_scr[...]


def _mlstm(proj, col, row, *, n_seq, seq_len, row0, init=None, emit_state=False):
    nc = seq_len // M_CHUNK
    blk0 = row0 // M_CHUNK

    def chunk(b, d, c):
        return blk0 + b * nc + jnp.where(d == 0, c, nc - 1 - c)

    in_specs = [
        pl.BlockSpec((M_CHUNK, M_WIDTH), lambda b, d, c: (chunk(b, d, c), COL_MQ // M_WIDTH)),
        pl.BlockSpec((M_CHUNK, M_WIDTH), lambda b, d, c: (chunk(b, d, c), COL_MK // M_WIDTH)),
        pl.BlockSpec((M_CHUNK, M_WIDTH), lambda b, d, c: (chunk(b, d, c), COL_MV // M_WIDTH)),
        pl.BlockSpec((None, M_CHUNK, GATE_PAD), lambda b, d, c: (d, chunk(b, d, c), 0)),
        pl.BlockSpec((None, None, 8, M_CHUNK), lambda b, d, c: (d, chunk(b, d, c), 0, 0)),
    ]
    args = [proj, proj, proj, col, row]
    if init is not None:
        in_specs += [
            pl.BlockSpec((None, None, M_HEADS, M_DK, M_DV), lambda b, d, c: (b, d, 0, 0, 0)),
            pl.BlockSpec((None, None, M_HEADS, 1, M_DK), lambda b, d, c: (b, d, 0, 0, 0)),
            pl.BlockSpec((None, None, M_HEADS, 1, 128), lambda b, d, c: (b, d, 0, 0, 0)),
        ]
        args += list(init)
    out_specs = [pl.BlockSpec((None, M_CHUNK, M_WIDTH),
                              lambda b, d, c: (d, chunk(b, d, c) - blk0, 0))]
    out_shape = [jax.ShapeDtypeStruct((2, n_seq * seq_len, M_WIDTH), F32)]
    if emit_state:
        out_specs += [
            pl.BlockSpec((None, None, M_HEADS, M_DK, M_DV), lambda b, d, c: (b, d, 0, 0, 0)),
            pl.BlockSpec((None, None, M_HEADS, 1, M_DK), lambda b, d, c: (b, d, 0, 0, 0)),
            pl.BlockSpec((None, None, M_HEADS, 1, 128), lambda b, d, c: (b, d, 0, 0, 0)),
        ]
        out_shape += [
            jax.ShapeDtypeStruct((n_seq, 2, M_HEADS, M_DK, M_DV), F32),
            jax.ShapeDtypeStruct((n_seq, 2, M_HEADS, 1, M_DK), F32),
            jax.ShapeDtypeStruct((n_seq, 2, M_HEADS, 1, 128), F32),
        ]
    return pl.pallas_call(
        functools.partial(_mlstm_kernel, has_init=init is not None, emit_state=emit_state),
        grid=(n_seq, 2, nc),
        in_specs=in_specs,
        out_specs=out_specs,
        out_shape=out_shape,
        scratch_shapes=[
            pltpu.VMEM((M_HEADS, M_DK, M_DV), F32),
            pltpu.VMEM((M_HEADS, 1, M_DK), F32),
            pltpu.VMEM((M_HEADS, 1, 128), F32),
        ],
        compiler_params=_cparams(("parallel", "parallel", "arbitrary"), 32),
        name="mlstm_scan_init" if init is not None else "mlstm_scan",
    )(*args)


def _swap32(x):
    lane = lax.broadcasted_iota(jnp.int32, x.shape, 1)
    return jnp.where((lane % 64) < 32, pltpu.roll(x, HEAD_DIM - 32, axis=1),
                     pltpu.roll(x, 32, axis=1))


def _softmax_pv(q, kb, vb):
    s = lax.dot_general(q, kb, (((1,), (1,)), ((), ())), preferred_element_type=F32)
    e = jnp.exp(s - jnp.max(s, axis=-1, keepdims=True))
    l = jnp.sum(e, axis=-1, keepdims=True)
    return jnp.dot(e.astype(BF16), vb, preferred_element_type=F32) / l


def _attn_ctx_kernel(q_ref, k_ref, v_ref, qw_ref, kw_ref, o_ref, kout_ref):
    k = _rms(k_ref[...]) * kw_ref[...]
    kout_ref[...] = k
    kb = k.astype(BF16)
    vb = v_ref[...].astype(BF16)
    for g in range(ATT_GROUP):
        cols = slice(g * HEAD_DIM, (g + 1) * HEAD_DIM)
        q = _rms(q_ref[:, cols]) * qw_ref[...] * (HEAD_DIM ** -0.5)
        o_ref[:, cols] = _softmax_pv(q.astype(BF16), kb, vb).astype(BF16)


def _attention_ctx(proj, qw, kw):
    gw = ATT_GROUP * HEAD_DIM
    return pl.pallas_call(
        _attn_ctx_kernel,
        grid=(BATCH, ATT_KV_HEADS),
        in_specs=[
            pl.BlockSpec((SEQ, gw), lambda b, h: (b, COL_AQ // gw + h)),
            pl.BlockSpec((SEQ, HEAD_DIM), lambda b, h: (b, COL_AK // HEAD_DIM + h)),
            pl.BlockSpec((SEQ, HEAD_DIM), lambda b, h: (b, COL_AV // HEAD_DIM + h)),
            pl.BlockSpec((1, HEAD_DIM), lambda b, h: (0, 0)),
            pl.BlockSpec((1, HEAD_DIM), lambda b, h: (0, 0)),
        ],
        out_specs=[
            pl.BlockSpec((SEQ, gw), lambda b, h: (b, h)),
            pl.BlockSpec((SEQ, HEAD_DIM), lambda b, h: (b, h)),
        ],
        out_shape=[
            jax.ShapeDtypeStruct((N_P, ATT_WIDTH), BF16),
            jax.ShapeDtypeStruct((N_P, KV_WIDTH), F32),
        ],
        compiler_params=_cparams(("parallel", "parallel"), 32),
        name="attention_ctx",
    )(proj, proj, proj, qw, kw)


def _attn_dec_kernel(q_ref, k_ref, v_ref, ck_ref, cv_ref, qw_ref, kw_ref,
                     cos_ref, sin_ref, cosq_ref, sinq_ref, o_ref, k_scr, v_scr):
    @pl.when(pl.program_id(2) == 0)
    def _():
        k_scr[0:PAST_LEN, :] = ck_ref[...].astype(BF16)
        v_scr[0:PAST_LEN, :] = cv_ref[...].astype(BF16)
        k = _rms(k_ref[...]) * kw_ref[...]
        k = k * cos_ref[...] + _swap32(k) * sin_ref[...]
        k_scr[PAST_LEN:, :] = k.astype(BF16)
        v_scr[PAST_LEN:, :] = v_ref[...].astype(BF16)

    kb = k_scr[...]
    vb = v_scr[...]
    for g in range(ATT_GROUP):
        cols = slice(g * HEAD_DIM, (g + 1) * HEAD_DIM)
        q = _rms(q_ref[:, cols]) * qw_ref[...]
        q = (q * cosq_ref[...] + _swap32(q) * sinq_ref[...]) * (HEAD_DIM ** -0.5)
        o_ref[:, cols] = _softmax_pv(q.astype(BF16), kb, vb).astype(BF16)


def _attention_dec(proj, ck, cv, qw, kw, cos, sin):
    tq = 128
    gw = ATT_GROUP * HEAD_DIM
    nq = DEC_SEQ // tq
    seq0 = N_P // DEC_SEQ
    return pl.pallas_call(
        _attn_dec_kernel,
        grid=(DEC_BATCH, ATT_KV_HEADS, nq),
        in_specs=[
            pl.BlockSpec((tq, gw), lambda b, h, i: (N_P // tq + b * nq + i, COL_AQ // gw + h)),
            pl.BlockSpec((DEC_SEQ, HEAD_DIM), lambda b, h, i: (seq0 + b, COL_AK // HEAD_DIM + h)),
            pl.BlockSpec((DEC_SEQ, HEAD_DIM), lambda b, h, i: (seq0 + b, COL_AV // HEAD_DIM + h)),
            pl.BlockSpec((None, PAST_LEN, HEAD_DIM), lambda b, h, i: (b, 0, h)),
            pl.BlockSpec((None, PAST_LEN, HEAD_DIM), lambda b, h, i: (b, 0, h)),
            pl.BlockSpec((1, HEAD_DIM), lambda b, h, i: (0, 0)),
            pl.BlockSpec((1, HEAD_DIM), lambda b, h, i: (0, 0)),
            pl.BlockSpec((DEC_SEQ, HEAD_DIM), lambda b, h, i: (0, 0)),
            pl.BlockSpec((DEC_SEQ, HEAD_DIM), lambda b, h, i: (0, 0)),
            pl.BlockSpec((tq, HEAD_DIM), lambda b, h, i: (i, 0)),
            pl.BlockSpec((tq, HEAD_DIM), lambda b, h, i: (i, 0)),
        ],
        out_specs=pl.BlockSpec((tq, gw), lambda b, h, i: (b * nq + i, h)),
        out_shape=jax.ShapeDtypeStruct((N_S, ATT_WIDTH), BF16),
        scratch_shapes=[
            pltpu.VMEM((KV_LEN_S, HEAD_DIM), BF16),
            pltpu.VMEM((KV_LEN_S, HEAD_DIM), BF16),
        ],
        compiler_params=_cparams(("parallel", "parallel", "arbitrary"), 32),
        name="attention_dec",
    )(proj, proj, proj, ck, cv, qw, kw, cos, sin, cos, sin)


def _rope_tables():
    t = jnp.arange(DEC_SEQ)
    inv = ROPE_THETA ** (-jnp.arange(ROPE_FREQS, dtype=F32) / ROPE_FREQS)
    ang = jnp.stack([t // GRID_W, t % GRID_W], axis=-1).astype(F32)[:, :, None] * inv
    cos = jnp.cos(ang)
    sin = jnp.sin(ang)
    cos = jnp.stack([cos, cos], axis=2).reshape(DEC_SEQ, HEAD_DIM)
    sin = jnp.stack([-sin, sin], axis=2).reshape(DEC_SEQ, HEAD_DIM)
    return cos, sin


def _merge_kernel(att_ref, hf_ref, hb_ref, mo_ref, cb_ref, cc_ref, cx_ref,
                  ccp_ref, cxp_ref, ccn_ref, cxn_ref, gl0_ref, gl1_ref, gl2_ref,
                  wb_ref, mn_ref, cw_ref, o_ref, *, tm):
    i = pl.program_id(0)
    row0 = i * tm
    in_p = row0 < N_P
    off = jnp.where(in_p, row0 % SEQ, (row0 - N_P) % DEC_SEQ)
    seq_len = jnp.where(in_p, SEQ, DEC_SEQ)
    first = off == 0
    final = off + tm == seq_len

    hsum = hf_ref[...] + hb_ref[...]
    mn = mn_ref[...]
    parts = []
    for hd in range(M_HEADS):
        cols = slice(hd * M_DV, (hd + 1) * M_DV)
        parts.append(_rms(hsum[:, cols]) * mn[:, cols])
    ml = jnp.concatenate(parts, axis=-1) * jax.nn.sigmoid(mo_ref[...])

    u = cc_ref[...] * cx_ref[...]
    u_prev_row = jnp.where(first, 0.0, ccp_ref[7:8, :] * cxp_ref[7:8, :])
    u_next_row = jnp.where(final, 0.0, ccn_ref[0:1, :] * cxn_ref[0:1, :])
    ridx = lax.broadcasted_iota(jnp.int32, u.shape, 0)
    u_prev = jnp.where(ridx == 0, u_prev_row, pltpu.roll(u, 1, axis=0))
    u_next = jnp.where(ridx == tm - 1, u_next_row, pltpu.roll(u, tm - 1, axis=0))
    cw = cw_ref[...]
    cv = cb_ref[...] * (cw[0:1, :] * u_prev + cw[1:2, :] * u + cw[2:3, :] * u_next)

    acc = jax.nn.sigmoid(gl0_ref[...]) * jnp.dot(att_ref[...], wb_ref[0],
                                                 preferred_element_type=F32)
    acc += jax.nn.sigmoid(gl1_ref[...]) * jnp.dot(ml.astype(BF16), wb_ref[1],
                                                  preferred_element_type=F32)
    acc += jax.nn.sigmoid(gl2_ref[...]) * jnp.dot(cv.astype(BF16), wb_ref[2],
                                                  preferred_element_type=F32)
    o_ref[...] = acc.astype(BF16)


def _merge(att, hdir, proj, wb, mnorm, convw):
    tm = 128
    nt = N_TOK // tm
    t8 = tm // 8
    last8 = N_TOK // 8 - 1
    cblk = lambda col: (lambda i: (i, col // BRANCH_W))
    prev8 = lambda col: (lambda i: (jnp.maximum(i * t8 - 1, 0), col // BRANCH_W))
    next8 = lambda col: (lambda i: (jnp.minimum((i + 1) * t8, last8), col // BRANCH_W))
    glblk = lambda g: (lambda i: (i, COL_GL // D_MODEL + g))
    return pl.pallas_call(
        functools.partial(_merge_kernel, tm=tm),
        grid=(nt,),
        in_specs=[
            pl.BlockSpec((tm, BRANCH_W), lambda i: (i, 0)),
            pl.BlockSpec((None, tm, BRANCH_W), lambda i: (0, i, 0)),
            pl.BlockSpec((None, tm, BRANCH_W), lambda i: (1, i, 0)),
            pl.BlockSpec((tm, BRANCH_W), cblk(COL_MO)),
            pl.BlockSpec((tm, BRANCH_W), cblk(COL_CB)),
            pl.BlockSpec((tm, BRANCH_W), cblk(COL_CC)),
            pl.BlockSpec((tm, BRANCH_W), cblk(COL_CX)),
            pl.BlockSpec((8, BRANCH_W), prev8(COL_CC)),
            pl.BlockSpec((8, BRANCH_W), prev8(COL_CX)),
            pl.BlockSpec((8, BRANCH_W), next8(COL_CC)),
            pl.BlockSpec((8, BRANCH_W), next8(COL_CX)),
            pl.BlockSpec((tm, D_MODEL), glblk(0)),
            pl.BlockSpec((tm, D_MODEL), glblk(1)),
            pl.BlockSpec((tm, D_MODEL), glblk(2)),
            pl.BlockSpec((N_BRANCH, BRANCH_W, D_MODEL), lambda i: (0, 0, 0)),
            pl.BlockSpec((1, M_WIDTH), lambda i: (0, 0)),
            pl.BlockSpec((8, CONV_WIDTH), lambda i: (0, 0)),
        ],
        out_specs=pl.BlockSpec((tm, D_MODEL), lambda i: (i, 0)),
        out_shape=jax.ShapeDtypeStruct((N_TOK, D_MODEL), BF16),
        compiler_params=_cparams(("parallel",), 48),
        name="branch_merge",
    )(att, hdir, hdir, proj, proj, proj, proj, proj, proj, proj, proj,
      proj, proj, proj, wb, mnorm, convw)


def _outproj_kernel(m_ref, w_ref, x_ref, gpost_ref, gate_ref, gpre_ref, sc_ref, sh_ref,
                    x_out_ref, h_out_ref):
    mix = jnp.dot(m_ref[...], w_ref[...], preferred_element_type=F32)
    x = x_ref[...] + gate_ref[...] * (_rms(mix) * gpost_ref[...])
    x_out_ref[...] = x
    h = (_rms(x) * gpre_ref[...]) * (1 + sc_ref[...]) + sh_ref[...]
    h_out_ref[...] = h.astype(BF16)


def _outproj(merged, w_out, x, gpost, gate, gpre, sc, sh):
    tm = 256
    row = lambda i: (_cond_row(i, tm), 0, 0)
    vec = pl.BlockSpec((1, D_MODEL), lambda i: (0, 0))
    cond = pl.BlockSpec((None, 1, D_MODEL), row)
    tile = pl.BlockSpec((tm, D_MODEL), lambda i: (i, 0))
    return pl.pallas_call(
        _outproj_kernel,
        grid=(N_TOK // tm,),
        in_specs=[tile, pl.BlockSpec((D_MODEL, D_MODEL), lambda i: (0, 0)), tile,
                  vec, cond, vec, cond, cond],
        out_specs=[tile, tile],
        out_shape=[jax.ShapeDtypeStruct((N_TOK, D_MODEL), F32),
                   jax.ShapeDtypeStruct((N_TOK, D_MODEL), BF16)],
        compiler_params=_cparams(("parallel",), 48),
        name="out_projection",
    )(merged, w_out, x, gpost, gate, gpre, sc, sh)


def _ffn_kernel(*refs, emit_next):
    h_ref, wg_ref, wu_ref, wo_ref, x_ref, gpost_ref, gate_ref = refs[:7]
    pos = 7
    if emit_next:
        gpre_ref, sc_ref, sh_ref = refs[pos:pos + 3]
        pos += 3
    x_out_ref = refs[pos]
    pos += 1
    if emit_next:
        h_out_ref = refs[pos]
        pos += 1
    acc_ref = refs[pos]

    j = pl.program_id(1)

    @pl.when(j == 0)
    def _():
        acc_ref[...] = jnp.zeros_like(acc_ref)

    h = h_ref[...]
    gate = jnp.dot(h, wg_ref[...], preferred_element_type=F32)
    up = jnp.dot(h, wu_ref[...], preferred_element_type=F32)
    act = (gate * jax.nn.sigmoid(gate) * up).astype(BF16)
    acc_ref[...] += jnp.dot(act, wo_ref[...], preferred_element_type=F32)

    @pl.when(j == pl.num_programs(1) - 1)
    def _():
        x = x_ref[...] + gate_ref[...] * (_rms(acc_ref[...]) * gpost_ref[...])
        x_out_ref[...] = x
        if emit_next:
            hn = (_rms(x) * gpre_ref[...]) * (1 + sc_ref[...]) + sh_ref[...]
            h_out_ref[...] = hn.astype(BF16)


def _ffn(h, w_in, w_out, x, gpost, gate, nxt=None):
    tm, th = 512, 512
    nj = FF_HIDDEN // th
    row = lambda i, j: (_cond_row(i, tm), 0, 0)
    vec = pl.BlockSpec((1, D_MODEL), lambda i, j: (0, 0))
    cond = pl.BlockSpec((None, 1, D_MODEL), row)
    tile = pl.BlockSpec((tm, D_MODEL), lambda i, j: (i, 0))
    in_specs = [
        tile,
        pl.BlockSpec((D_MODEL, th), lambda i, j: (0, j)),
        pl.BlockSpec((D_MODEL, th), lambda i, j: (0, nj + j)),
        pl.BlockSpec((th, D_MODEL), lambda i, j: (j, 0)),
        tile, vec, cond,
    ]
    args = [h, w_in, w_in, w_out, x, gpost, gate]
    out_specs = [tile]
    out_shape = [jax.ShapeDtypeStruct((N_TOK, D_MODEL), F32)]
    if nxt is not None:
        in_specs += [vec, cond, cond]
        args += list(nxt)
        out_specs.append(tile)
        out_shape.append(jax.ShapeDtypeStruct((N_TOK, D_MODEL), BF16))
    return pl.pallas_call(
        functools.partial(_ffn_kernel, emit_next=nxt is not None),
        grid=(N_TOK // tm, nj),
        in_specs=in_specs,
        out_specs=out_specs,
        out_shape=out_shape,
        scratch_shapes=[pltpu.VMEM((tm, D_MODEL), F32)],
        compiler_params=_cparams(("parallel", "arbitrary"), 48),
        name="ffn",
    )(*args)


def kernel(x_prompt, x_sample, cache_k, cache_v, state_C, state_n, state_m, c, c_ctx, w_mod, b_mod, norm_pre1, norm_post1, norm_pre2, norm_post2, w_in, q_norm, k_norm, mlstm_gate_bias, mlstm_norm, conv_w, w_branch, w_out, w_ffn_in, w_ffn_out):
    cond = jnp.concatenate([c_ctx[None], c, jnp.zeros((N_COND - 1 - DEC_BATCH, D_MODEL), F32)], axis=0)
    mod = _modulation(cond, w_mod, b_mod)
    mod = mod.reshape(DEPTH, N_COND, 6, 1, D_MODEL).transpose(0, 2, 1, 3, 4)
    vec = lambda a: a.reshape(1, -1)
    cos, sin = _rope_tables()

    x = jnp.concatenate([x_prompt.reshape(N_P, D_MODEL), x_sample.reshape(N_S, D_MODEL)], axis=0)
    h = _prenorm(x, vec(norm_pre1[0]), mod[0, 1], mod[0, 0])

    o_aq, o_ak, o_av = 0, 1024, 1280
    o_mq, o_mk, o_mv, o_mo, o_mg = 1536, 2560, 3584, 4608, 5632
    o_cb, o_cc, o_cx, o_gl = 5648, 6672, 7696, 8720

    ks, vs, cs, ns, ms = [], [], [], [], []
    for l in range(DEPTH):
        sh1, sc1, g1, sh2, sc2, g2 = (mod[l, i] for i in range(6))
        wl = w_in[l]
        seg = lambda o, n: wl[:, o:o + n]
        w_proj = jnp.concatenate(
            [seg(o_mq, 1024), seg(o_mk, 1024), seg(o_mv, 1024), seg(o_mo, 1024),
             seg(o_aq, 1024), seg(o_cb, 1024), seg(o_cc, 1024), seg(o_cx, 1024),
             seg(o_gl, N_BRANCH * D_MODEL), seg(o_ak, KV_WIDTH), seg(o_av, KV_WIDTH)],
            axis=1).astype(BF16)
        w_gate = jnp.pad(seg(o_mg, 4 * M_HEADS), ((0, 0), (0, GATE_PAD - 4 * M_HEADS))).astype(BF16)
        gate_bias = jnp.pad(mlstm_gate_bias[l], (0, GATE_PAD - 4 * M_HEADS)).reshape(1, GATE_PAD)

        proj = _projection(h, w_proj)
        col, row = _gates(h, w_gate, gate_bias)

        qw, kw = vec(q_norm[l]), vec(k_norm[l])
        att_p, k_new = _attention_ctx(proj, qw, kw)
        att_s = _attention_dec(proj, cache_k[:, l].reshape(DEC_BATCH, PAST_LEN, KV_WIDTH),
                               cache_v[:, l].reshape(DEC_BATCH, PAST_LEN, KV_WIDTH),
                               qw, kw, cos, sin)
        att = jnp.concatenate([att_p, att_s], axis=0)

        hdir_p, c_fin, n_fin, m_fin = _mlstm(proj, col, row, n_seq=BATCH, seq_len=SEQ, row0=0,
                                             emit_state=True)
        init = (state_C[:, l],
                state_n[:, l].reshape(DEC_BATCH, 2, M_HEADS, 1, M_DK),
                jnp.broadcast_to(state_m[:, l].reshape(DEC_BATCH, 2, M_HEADS, 1, 1),
                                 (DEC_BATCH, 2, M_HEADS, 1, 128)))
        (hdir_s,) = _mlstm(proj, col, row, n_seq=DEC_BATCH, seq_len=DEC_SEQ, row0=N_P, init=init)
        hdir = jnp.concatenate([hdir_p, hdir_s], axis=1)

        merged = _merge(att, hdir, proj, w_branch[l].astype(BF16), vec(mlstm_norm[l]),
                        jnp.pad(conv_w[l], ((0, 5), (0, 0))))
        x, h2 = _outproj(merged, w_out[l].astype(BF16), x, vec(norm_post1[l]), g1,
                         vec(norm_pre2[l]), sc2, sh2)
        if l + 1 < DEPTH:
            nxt = (vec(norm_pre1[l + 1]), mod[l + 1, 1], mod[l + 1, 0])
            x, h = _ffn(h2, w_ffn_in[l].astype(BF16), w_ffn_out[l].astype(BF16), x,
                        vec(norm_post2[l]), g2, nxt)
        else:
            (x,) = _ffn(h2, w_ffn_in[l].astype(BF16), w_ffn_out[l].astype(BF16), x,
                        vec(norm_post2[l]), g2)

        ks.append(k_new.reshape(BATCH, SEQ, ATT_KV_HEADS, HEAD_DIM))
        vs.append(proj[:N_P, COL_AV:COL_AV + KV_WIDTH].reshape(BATCH, SEQ, ATT_KV_HEADS, HEAD_DIM))
        cs.append(c_fin)
        ns.append(n_fin.reshape(BATCH, 2, M_HEADS, M_DK))
        ms.append(m_fin[:, :, :, 0, 0])

    y_prompt = x[:N_P].reshape(BATCH, SEQ, D_MODEL)
    y_sample = x[N_P:].reshape(DEC_BATCH, DEC_SEQ, D_MODEL)
    return (y_prompt, y_sample, jnp.stack(ks, axis=1), jnp.stack(vs, axis=1),
            jnp.stack(cs, axis=1), jnp.stack(ns, axis=1), jnp.stack(ms, axis=1))
```

```python
import functools

import jax
import jax.numpy as jnp
from jax import lax
from jax.experimental import pallas as pl
from jax.experimental.pallas import tpu as pltpu

F32 = jnp.float32
BF16 = jnp.bfloat16

D_MODEL = 2048
BATCH = 16
SEQ = 256
DEPTH = 2
DEC_BATCH = 2
DEC_SEQ = 2048
PAST_LEN = 256
GRID_W = 64
EPS = 1e-6
HEAD_DIM = 128
ATT_Q_HEADS = 8
ATT_KV_HEADS = 2
ATT_GROUP = ATT_Q_HEADS // ATT_KV_HEADS
ATT_WIDTH = ATT_Q_HEADS * HEAD_DIM
KV_WIDTH = ATT_KV_HEADS * HEAD_DIM
ROPE_THETA = 10000.0
ROPE_FREQS = HEAD_DIM // 4
M_HEADS = 4
M_DK = 256
M_DV = 256
M_WIDTH = M_HEADS * M_DV
M_CHUNK = 128
CONV_WIDTH = 1024
N_BRANCH = 3
BRANCH_W = 1024
FF_HIDDEN = 5632
IN_WIDTH = 14864

LANES = 128
N_P = BATCH * SEQ
N_S = DEC_BATCH * DEC_SEQ
N_TOK = N_P + N_S
N_COND = 8
KV_LEN_S = PAST_LEN + DEC_SEQ

W_AQ, W_AK, W_AV = 0, 1024, 1280
W_MQ, W_MG, W_CB = 1536, 5632, 5648
GATE_COLS = 4 * M_HEADS
PROJ_TN = 512
N_TILE_A = W_MG // PROJ_TN
N_TILE_C = (IN_WIDTH - W_CB) // PROJ_TN
N_TILE_32 = W_MQ // PROJ_TN

P32_W = W_MQ
C16_GL, C16_MQ, C16_MK, C16_MV, C16_MO = 0, 6144, 7168, 8192, 9216
C16_CB, C16_CC, C16_CX = 10240, 11264, 12288
P16_W = 13312
M_AUG = M_DV + LANES


def _cparams(semantics, vmem_mb):
    return pltpu.CompilerParams(dimension_semantics=semantics,
                                vmem_limit_bytes=vmem_mb * 1024 * 1024)


def _rms(x):
    return x * lax.rsqrt(jnp.mean(x * x, axis=-1, keepdims=True) + EPS)


def _cond_row(tile, tm):
    return jnp.where(tile < N_P // tm, 0, 1 + (tile * tm - N_P) // DEC_SEQ)


def _mod_kernel(c_ref, w_ref, b_ref, o_ref):
    c = c_ref[...]
    a = (c * jax.nn.sigmoid(c)).astype(BF16)
    o_ref[...] = jnp.dot(a, w_ref[...].astype(BF16), preferred_element_type=F32) + b_ref[...]


def _modulation(cond, w_mod, b_mod):
    tn = 1024
    n = 6 * D_MODEL
    return pl.pallas_call(
        _mod_kernel,
        grid=(DEPTH, n // tn),
        in_specs=[
            pl.BlockSpec((N_COND, D_MODEL), lambda l, j: (0, 0)),
            pl.BlockSpec((None, D_MODEL, tn), lambda l, j: (l, 0, j)),
            pl.BlockSpec((None, 1, tn), lambda l, j: (l, 0, j)),
        ],
        out_specs=pl.BlockSpec((None, N_COND, tn), lambda l, j: (l, 0, j)),
        out_shape=jax.ShapeDtypeStruct((DEPTH, N_COND, n), F32),
        compiler_params=_cparams(("parallel", "parallel"), 32),
        name="modulation",
    )(cond, w_mod, b_mod.reshape(DEPTH, 1, n))


def _prenorm_kernel(xp_ref, xs_ref, g_ref, sc_ref, sh_ref, x_ref, h_ref, *, n_p_tiles):
    def emit(src_ref):
        x = src_ref[...]
        x_ref[...] = x
        y = _rms(x) * g_ref[...]
        h_ref[...] = (y * (1 + sc_ref[...]) + sh_ref[...]).astype(BF16)

    i = pl.program_id(0)
    pl.when(i < n_p_tiles)(lambda: emit(xp_ref))
    pl.when(i >= n_p_tiles)(lambda: emit(xs_ref))


def _prenorm(xp, xs, g, sc, sh):
    tm = 512
    npt = N_P // tm
    row = lambda i: (_cond_row(i, tm), 0, 0)
    tile = pl.BlockSpec((tm, D_MODEL), lambda i: (i, 0))
    return pl.pallas_call(
        functools.partial(_prenorm_kernel, n_p_tiles=npt),
        grid=(N_TOK // tm,),
        in_specs=[
            pl.BlockSpec((tm, D_MODEL), lambda i: (jnp.minimum(i, npt - 1), 0)),
            pl.BlockSpec((tm, D_MODEL), lambda i: (jnp.maximum(i - npt, 0), 0)),
            pl.BlockSpec((1, D_MODEL), lambda i: (0, 0)),
            pl.BlockSpec((None, 1, D_MODEL), row),
            pl.BlockSpec((None, 1, D_MODEL), row),
        ],
        out_specs=[tile, tile],
        out_shape=[jax.ShapeDtypeStruct((N_TOK, D_MODEL), F32),
                   jax.ShapeDtypeStruct((N_TOK, D_MODEL), BF16)],
        compiler_params=_cparams(("arbitrary",), 32),
        name="prenorm",
    )(xp, xs, g, sc, sh)


def _proj_tile16(j):
    n_m = N_TILE_A - N_TILE_32
    n_c = (C16_MQ - C16_GL) // PROJ_TN
    first_c = N_TILE_A + 3 * CONV_WIDTH // PROJ_TN
    return jnp.where(j < N_TILE_32, C16_MQ // PROJ_TN,
                     jnp.where(j < N_TILE_A, C16_MQ // PROJ_TN + (j - N_TILE_32),
                               jnp.where(j < first_c, C16_CB // PROJ_TN + (j - N_TILE_A),
                                         jnp.minimum(j - first_c, n_c - 1))))


def _proj_kernel(x_ref, w_ref, tail_ref, o32_ref, o16_ref, w_scr):
    j = pl.program_id(1)

    @pl.when(j < N_TILE_A)
    def _():
        w_scr[...] = w_ref[...].astype(BF16)

    @pl.when(j >= N_TILE_A)
    def _():
        lane = lax.broadcasted_iota(jnp.int32, (D_MODEL, LANES), 1)
        groups = PROJ_TN // LANES
        for g in range(groups):
            cur = w_ref[:, g * LANES:(g + 1) * LANES]
            nxt = tail_ref[...] if g == groups - 1 else w_ref[:, (g + 1) * LANES:(g + 2) * LANES]
            shifted = jnp.where(lane < LANES - GATE_COLS,
                                pltpu.roll(cur, LANES - GATE_COLS, axis=1),
                                pltpu.roll(nxt, LANES - GATE_COLS, axis=1))
            w_scr[:, g * LANES:(g + 1) * LANES] = shifted.astype(BF16)

    acc = jnp.dot(x_ref[...], w_scr[...], preferred_element_type=F32)

    @pl.when(j < N_TILE_32)
    def _():
        o32_ref[...] = acc

    @pl.when(j >= N_TILE_32)
    def _():
        o16_ref[...] = acc.astype(BF16)


def _projection(h, w_in, layer):
    tm = 2048
    nj = N_TILE_A + N_TILE_C
    lanes_per_tile = PROJ_TN // LANES
    tail0 = W_MG // LANES

    def tail_idx(i, j):
        return (layer, 0, jnp.where(j < N_TILE_A, tail0, (j + 1) * lanes_per_tile))

    return pl.pallas_call(
        _proj_kernel,
        grid=(N_TOK // tm, nj),
        in_specs=[
            pl.BlockSpec((tm, D_MODEL), lambda i, j: (i, 0), pipeline_mode=pl.Buffered(1)),
            pl.BlockSpec((None, D_MODEL, PROJ_TN), lambda i, j: (layer, 0, j)),
            pl.BlockSpec((None, D_MODEL, LANES), tail_idx),
        ],
        out_specs=[
            pl.BlockSpec((tm, PROJ_TN), lambda i, j: (i, jnp.minimum(j, N_TILE_32 - 1))),
            pl.BlockSpec((tm, PROJ_TN), lambda i, j: (i, _proj_tile16(j))),
        ],
        out_shape=[jax.ShapeDtypeStruct((N_TOK, P32_W), F32),
                   jax.ShapeDtypeStruct((N_TOK, P16_W), BF16)],
        scratch_shapes=[pltpu.VMEM((D_MODEL, PROJ_TN), BF16)],
        compiler_params=_cparams(("arbitrary", "arbitrary"), 52),
        name="projection",
    )(h, w_in, w_in)


def _gate_kernel(h_ref, w_ref, b_ref, col_ref, row_ref, *, chunks):
    g = jnp.dot(h_ref[...], w_ref[...].astype(BF16), preferred_element_type=F32) + b_ref[...]
    lf = jax.nn.log_sigmoid(g)
    r = lax.broadcasted_iota(jnp.int32, (M_CHUNK, M_CHUNK), 0)
    c = lax.broadcasted_iota(jnp.int32, (M_CHUNK, M_CHUNK), 1)
    tril = (c <= r).astype(F32)
    triu = (c >= r).astype(F32)
    lane = lax.broadcasted_iota(jnp.int32, (M_CHUNK, LANES), 1)
    for ch in range(chunks):
        rows = slice(ch * M_CHUNK, (ch + 1) * M_CHUNK)
        gc = g[rows]
        lfc = lf[rows]
        pre = jnp.dot(tril, lfc, preferred_element_type=F32, precision=lax.Precision.HIGHEST)
        suf = jnp.dot(triu, lfc, preferred_element_type=F32, precision=lax.Precision.HIGHEST)
        both = jnp.where((lane >= 4) & (lane < 8), pre,
                         jnp.where((lane >= 12) & (lane < 16), suf, gc))
        bwd = pltpu.roll(both, LANES - 8, axis=1)
        col_ref[0, rows, :] = both
        col_ref[1, rows, :] = bwd
        row_ref[0, ch] = both.T[0:8, :]
        row_ref[1, ch] = bwd.T[0:8, :]


def _gates(h, w_in, layer, bias):
    tm = 512
    chunks = tm // M_CHUNK
    return pl.pallas_call(
        functools.partial(_gate_kernel, chunks=chunks),
        grid=(N_TOK // tm,),
        in_specs=[
            pl.BlockSpec((tm, D_MODEL), lambda i: (i, 0)),
            pl.BlockSpec((None, D_MODEL, LANES), lambda i: (layer, 0, W_MG // LANES)),
            pl.BlockSpec((1, LANES), lambda i: (0, 0)),
        ],
        out_specs=[
            pl.BlockSpec((2, tm, LANES), lambda i: (0, i, 0)),
            pl.BlockSpec((2, chunks, 8, M_CHUNK), lambda i: (0, i, 0, 0)),
        ],
        out_shape=[
            jax.ShapeDtypeStruct((2, N_TOK, LANES), F32),
            jax.ShapeDtypeStruct((2, N_TOK // M_CHUNK, 8, M_CHUNK), F32),
        ],
        compiler_params=_cparams(("parallel",), 32),
        name="mlstm_gates",
    )(h, w_in, bias)


def _mlstm_kernel(*refs, n_par, has_init, emit_state, has_prev):
    pos = 0
    seq_refs = []
    for _ in range(n_par):
        seq_refs.append(refs[pos:pos + 5])
        pos += 5
    if has_init:
        c0_ref, n0_ref, m0_ref = refs[pos:pos + 3]
        pos += 3
    if has_prev:
        pos += 3
    h_ref = refs[pos]
    pos += 1
    if emit_state:
        cout_ref, nout_ref, mout_ref = refs[pos:pos + 3]
        pos += 3
    c_scr, m_scr = refs[pos:pos + 2]

    d = pl.program_id(1)
    ci = pl.program_id(2)
    last = pl.num_programs(2) - 1
    lane_aug = lax.broadcasted_iota(jnp.int32, (M_CHUNK, LANES), 1)
    one_col = (lane_aug == 0).astype(F32)

    @pl.when(ci == 0)
    def _():
        if has_init:
            lane_n = lax.broadcasted_iota(jnp.int32, (M_DK, LANES), 1)
            for u in range(n_par):
                for hd in range(M_HEADS):
                    c_scr[u, hd, :, 0:M_DV] = c0_ref[u, hd]
                    n_col = jnp.broadcast_to(n0_ref[u, hd], (LANES, M_DK)).T
                    c_scr[u, hd, :, M_DV:M_AUG] = jnp.where(lane_n == 0, n_col, 0.0)
            m_scr[...] = m0_ref[...]
        else:
            c_scr[...] = jnp.zeros_like(c_scr)
            m_scr[...] = jnp.zeros_like(m_scr)

    t_idx = lax.broadcasted_iota(jnp.int32, (M_CHUNK, M_CHUNK), 0)
    s_idx = lax.broadcasted_iota(jnp.int32, (M_CHUNK, M_CHUNK), 1)
    mask = (s_idx - t_idx) * (1 - 2 * d) <= 0
    fwd = d == 0

    for u in range(n_par):
        q_ref, k_ref, v_ref, col_ref, row_ref = seq_refs[u]
        col = col_ref[...]
        row = row_ref[...]
        for hd in range(M_HEADS):
            cols = slice(hd * M_DK, (hd + 1) * M_DK)
            q = q_ref[:, cols]
            k = k_ref[:, cols] * (M_DK ** -0.5)
            v32 = v_ref[:, cols].astype(F32)
            i_col = col[:, hd:hd + 1]
            b_col = col[:, 4 + hd:5 + hd]
            i_row = row[hd:hd + 1, :]
            b_row = row[4 + hd:5 + hd, :]
            bl = jnp.where(fwd, b_row[:, M_CHUNK - 1:M_CHUNK], b_row[:, 0:1])
            mem = c_scr[u, hd]
            m_prev = m_scr[u, hd][:, 0:1]

            logd = jnp.where(mask, b_col - b_row + i_row, -jnp.inf)
            g = b_col + m_prev
            mt = jnp.maximum(g, jnp.max(logd, axis=-1, keepdims=True))
            qk = lax.dot_general(q, k, (((1,), (1,)), ((), ())), preferred_element_type=F32)
            s = qk * jnp.exp(logd - mt)
            inter = jnp.exp(g - mt)
            v_aug = jnp.concatenate([v32, one_col], axis=1).astype(BF16)
            nd = inter * jnp.dot(q, mem.astype(BF16), preferred_element_type=F32) \
                + jnp.dot(s.astype(BF16), v_aug, preferred_element_type=F32)
            den = nd[:, M_DV:M_DV + 1]
            h_ref[u, :, cols] = nd[:, 0:M_DV] / jnp.maximum(jnp.abs(den), jnp.exp(-mt))

            wl_row = bl - b_row + i_row
            wl_col = bl - b_col + i_col
            m_new = jnp.maximum(bl + m_prev, jnp.max(wl_row, axis=-1, keepdims=True))
            w_col = jnp.exp(wl_col - m_new)
            dec = jnp.exp(bl + m_prev - m_new)
            wv_aug = (w_col * jnp.concatenate([v32, one_col], axis=1)).astype(BF16)
            c_scr[u, hd] = dec * mem + lax.dot_general(k, wv_aug, (((0,), (0,)), ((), ())),
                                                       preferred_element_type=F32)
            m_scr[u, hd] = jnp.broadcast_to(m_new, (1, LANES))

    if emit_state:
        @pl.when(ci == last)
        def _():
            for u in range(n_par):
                for hd in range(M_HEADS):
                    cout_ref[u, hd] = c_scr[u, hd, :, 0:M_DV]
                    nout_ref[u, hd] = c_scr[u, hd, :, M_DV:M_AUG].T[0:1, :]
            mout_ref[...] = m_scr[...]


def _mlstm(p16, col, row, *, n_seq, seq_len, row0, layer, init=None, prev_state=None,
           emit_state=False):
    n_par = 2
    nc = seq_len // M_CHUNK
    blk0 = row0 // M_CHUNK

    def chunk_of(c, d):
        return jnp.where(d == 0, c, nc - 1 - c)

    in_specs, args = [], []
    for u in range(n_par):
        blk = lambda g, d, c, u=u: blk0 + (g * n_par + u) * nc + chunk_of(c, d)
        in_specs += [
            pl.BlockSpec((M_CHUNK, M_WIDTH), lambda g, d, c, blk=blk: (blk(g, d, c), C16_MQ // M_WIDTH)),
            pl.BlockSpec((M_CHUNK, M_WIDTH), lambda g, d, c, blk=blk: (blk(g, d, c), C16_MK // M_WIDTH)),
            pl.BlockSpec((M_CHUNK, M_WIDTH), lambda g, d, c, blk=blk: (blk(g, d, c), C16_MV // M_WIDTH)),
            pl.BlockSpec((None, M_CHUNK, LANES), lambda g, d, c, blk=blk: (d, blk(g, d, c), 0)),
            pl.BlockSpec((None, None, 8, M_CHUNK), lambda g, d, c, blk=blk: (d, blk(g, d, c), 0, 0)),
        ]
        args += [p16, p16, p16, col, row]
    if init is not None:
        in_specs += [
            pl.BlockSpec((n_par, None, None, M_HEADS, M_DK, M_DV), lambda g, d, c: (g, layer, d, 0, 0, 0)),
            pl.BlockSpec((n_par, None, None, M_HEADS, 1, M_DK), lambda g, d, c: (g, layer, d, 0, 0, 0)),
            pl.BlockSpec((n_par, None, M_HEADS, 1, LANES), lambda g, d, c: (g, d, 0, 0, 0)),
        ]
        args += list(init)
    aliases = {}
    if prev_state is not None:
        for k_, a in enumerate(prev_state):
            aliases[len(args)] = 1 + k_
            in_specs.append(pl.BlockSpec(memory_space=pl.ANY))
            args.append(a)
    out_specs = [pl.BlockSpec((None, n_par, None, M_CHUNK, M_WIDTH),
                              lambda g, d, c: (d, g, chunk_of(c, d), 0, 0))]
    out_shape = [jax.ShapeDtypeStruct((2, n_seq, nc, M_CHUNK, M_WIDTH), F32)]
    if emit_state:
        out_specs += [
            pl.BlockSpec((n_par, None, None, M_HEADS, M_DK, M_DV), lambda g, d, c: (g, layer, d, 0, 0, 0)),
            pl.BlockSpec((n_par, None, None, M_HEADS, 1, M_DK), lambda g, d, c: (g, layer, d, 0, 0, 0)),
            pl.BlockSpec((n_par, None, None, M_HEADS, 1, LANES), lambda g, d, c: (g, layer, d, 0, 0, 0)),
        ]
        out_shape += [
            jax.ShapeDtypeStruct((n_seq, DEPTH, 2, M_HEADS, M_DK, M_DV), F32),
            jax.ShapeDtypeStruct((n_seq, DEPTH, 2, M_HEADS, 1, M_DK), F32),
            jax.ShapeDtypeStruct((n_seq, DEPTH, 2, M_HEADS, 1, LANES), F32),
        ]
    return pl.pallas_call(
        functools.partial(_mlstm_kernel, n_par=n_par, has_init=init is not None,
                          emit_state=emit_state, has_prev=prev_state is not None),
        grid=(n_seq // n_par, 2, nc),
        in_specs=in_specs,
        out_specs=out_specs,
        out_shape=out_shape,
        input_output_aliases=aliases,
        scratch_shapes=[
            pltpu.VMEM((n_par, M_HEADS, M_DK, M_AUG), F32),
            pltpu.VMEM((n_par, M_HEADS, 1, LANES), F32),
        ],
        compiler_params=_cparams(("arbitrary", "arbitrary", "arbitrary"), 40),
        name="mlstm_scan_dec" if init is not None else "mlstm_scan_ctx",
    )(*args)


def _swap32(x):
    lane = lax.broadcasted_iota(jnp.int32, x.shape, 1)
    return jnp.where((lane % 64) < 32, pltpu.roll(x, HEAD_DIM - 32, axis=1),
                     pltpu.roll(x, 32, axis=1))


def _softmax_pv(q, kb, vb):
    s = lax.dot_general(q, kb, (((1,), (1,)), ((), ())), preferred_element_type=F32)
    e = jnp.exp(s - jnp.max(s, axis=-1, keepdims=True))
    l = jnp.sum(e, axis=-1, keepdims=True)
    return jnp.dot(e.astype(BF16), vb, preferred_element_type=F32) / l


def _attn_ctx_kernel(*refs, has_prev):
    q_ref, k_ref, v_ref, qw_ref, kw_ref = refs[:5]
    o_ref, kout_ref, vout_ref = refs[-3:]
    k = _rms(k_ref[...]) * kw_ref[...]
    v = v_ref[...]
    kout_ref[...] = k
    vout_ref[...] = v
    kb = k.astype(BF16)
    vb = v.astype(BF16)
    for g in range(ATT_GROUP):
        cols = slice(g * HEAD_DIM, (g + 1) * HEAD_DIM)
        q = _rms(q_ref[:, cols]) * qw_ref[...] * (HEAD_DIM ** -0.5)
        o_ref[:, cols] = _softmax_pv(q.astype(BF16), kb, vb).astype(BF16)


def _attention_ctx(p32, qw, kw, layer, prev_cache=None):
    gw = ATT_GROUP * HEAD_DIM
    in_specs = [
        pl.BlockSpec((SEQ, gw), lambda b, h: (b, W_AQ // gw + h)),
        pl.BlockSpec((SEQ, HEAD_DIM), lambda b, h: (b, W_AK // HEAD_DIM + h)),
        pl.BlockSpec((SEQ, HEAD_DIM), lambda b, h: (b, W_AV // HEAD_DIM + h)),
        pl.BlockSpec((1, HEAD_DIM), lambda b, h: (0, 0)),
        pl.BlockSpec((1, HEAD_DIM), lambda b, h: (0, 0)),
    ]
    args = [p32, p32, p32, qw, kw]
    aliases = {}
    if prev_cache is not None:
        for k_, a in enumerate(prev_cache):
            aliases[len(args)] = 1 + k_
            in_specs.append(pl.BlockSpec(memory_space=pl.ANY))
            args.append(a)
    cache_spec = pl.BlockSpec((None, None, SEQ, HEAD_DIM), lambda b, h: (b, layer, 0, h))
    cache_shape = jax.ShapeDtypeStruct((BATCH, DEPTH, SEQ, KV_WIDTH), F32)
    return pl.pallas_call(
        functools.partial(_attn_ctx_kernel, has_prev=prev_cache is not None),
        grid=(BATCH, ATT_KV_HEADS),
        in_specs=in_specs,
        out_specs=[pl.BlockSpec((SEQ, gw), lambda b, h: (b, h)), cache_spec, cache_spec],
        out_shape=[jax.ShapeDtypeStruct((N_P, ATT_WIDTH), BF16), cache_shape, cache_shape],
        input_output_aliases=aliases,
        compiler_params=_cparams(("arbitrary", "arbitrary"), 32),
        name="attention_ctx",
    )(*args)


def _attn_dec_kernel(q_ref, k_ref, v_ref, ck_ref, cv_ref, qw_ref, kw_ref,
                     cos_ref, sin_ref, cosq_ref, sinq_ref, o_ref, k_scr, v_scr):
    @pl.when(pl.program_id(2) == 0)
    def _():
        k_scr[0:PAST_LEN, :] = ck_ref[...].astype(BF16)
        v_scr[0:PAST_LEN, :] = cv_ref[...].astype(BF16)
        k = _rms(k_ref[...]) * kw_ref[...]
        k = k * cos_ref[...] + _swap32(k) * sin_ref[...]
        k_scr[PAST_LEN:, :] = k.astype(BF16)
        v_scr[PAST_LEN:, :] = v_ref[...].astype(BF16)

    kb = k_scr[...]
    vb = v_scr[...]
    for g in range(ATT_GROUP):
        cols = slice(g * HEAD_DIM, (g + 1) * HEAD_DIM)
        q = _rms(q_ref[:, cols]) * qw_ref[...]
        q = (q * cosq_ref[...] + _swap32(q) * sinq_ref[...]) * (HEAD_DIM ** -0.5)
        o_ref[:, cols] = _softmax_pv(q.astype(BF16), kb, vb).astype(BF16)


def _attention_dec(p32, ck, cv, qw, kw, cos, sin, layer):
    tq = 128
    gw = ATT_GROUP * HEAD_DIM
    nq = DEC_SEQ // tq
    seq0 = N_P // DEC_SEQ
    cache_spec = pl.BlockSpec((None, None, PAST_LEN, HEAD_DIM), lambda b, h, i: (b, layer, 0, h))
    return pl.pallas_call(
        _attn_dec_kernel,
        grid=(DEC_BATCH, ATT_KV_HEADS, nq),
        in_specs=[
            pl.BlockSpec((tq, gw), lambda b, h, i: (N_P // tq + b * nq + i, W_AQ // gw + h)),
            pl.BlockSpec((DEC_SEQ, HEAD_DIM), lambda b, h, i: (seq0 + b, W_AK // HEAD_DIM + h)),
            pl.BlockSpec((DEC_SEQ, HEAD_DIM), lambda b, h, i: (seq0 + b, W_AV // HEAD_DIM + h)),
            cache_spec, cache_spec,
            pl.BlockSpec((1, HEAD_DIM), lambda b, h, i: (0, 0)),
            pl.BlockSpec((1, HEAD_DIM), lambda b, h, i: (0, 0)),
            pl.BlockSpec((DEC_SEQ, HEAD_DIM), lambda b, h, i: (0, 0)),
            pl.BlockSpec((DEC_SEQ, HEAD_DIM), lambda b, h, i: (0, 0)),
            pl.BlockSpec((tq, HEAD_DIM), lambda b, h, i: (i, 0)),
            pl.BlockSpec((tq, HEAD_DIM), lambda b, h, i: (i, 0)),
        ],
        out_specs=pl.BlockSpec((tq, gw), lambda b, h, i: (b * nq + i, h)),
        out_shape=jax.ShapeDtypeStruct((N_S, ATT_WIDTH), BF16),
        scratch_shapes=[
            pltpu.VMEM((KV_LEN_S, HEAD_DIM), BF16),
            pltpu.VMEM((KV_LEN_S, HEAD_DIM), BF16),
        ],
        compiler_params=_cparams(("arbitrary", "arbitrary", "arbitrary"), 32),
        name="attention_dec",
    )(p32, p32, p32, ck, cv, qw, kw, cos, sin, cos, sin)


def _rope_tables():
    t = jnp.arange(DEC_SEQ)
    inv = ROPE_THETA ** (-jnp.arange(ROPE_FREQS, dtype=F32) / ROPE_FREQS)
    ang = jnp.stack([t // GRID_W, t % GRID_W], axis=-1).astype(F32)[:, :, None] * inv
    cos = jnp.cos(ang)
    sin = jnp.sin(ang)
    cos = jnp.stack([cos, cos], axis=2).reshape(DEC_SEQ, HEAD_DIM)
    sin = jnp.stack([-sin, sin], axis=2).reshape(DEC_SEQ, HEAD_DIM)
    return cos, sin


def _merge_kernel(attp_ref, atts_ref, hfp_ref, hbp_ref, hfs_ref, hbs_ref,
                  mo_ref, cb_ref, cc_ref, cx_ref, ccp_ref, cxp_ref, ccn_ref, cxn_ref,
                  gl0_ref, gl1_ref, gl2_ref, wb_ref, mn_ref, cw_ref, o_ref, *, tm, halo):
    i = pl.program_id(0)
    row0 = i * tm
    in_p = row0 < N_P
    off = jnp.where(in_p, row0 % SEQ, (row0 - N_P) % DEC_SEQ)
    seq_len = jnp.where(in_p, SEQ, DEC_SEQ)
    first = off == 0
    final = off + tm == seq_len

    hsum = jnp.where(in_p, hfp_ref[...] + hbp_ref[...], hfs_ref[...] + hbs_ref[...])
    mn = mn_ref[...]
    parts = []
    for hd in range(M_HEADS):
        cols = slice(hd * M_DV, (hd + 1) * M_DV)
        parts.append(_rms(hsum[:, cols]) * mn[:, cols])
    ml = jnp.concatenate(parts, axis=-1) * jax.nn.sigmoid(mo_ref[...].astype(F32))

    u = cc_ref[...].astype(F32) * cx_ref[...].astype(F32)
    u_halo_prev = ccp_ref[...].astype(F32) * cxp_ref[...].astype(F32)
    u_halo_next = ccn_ref[...].astype(F32) * cxn_ref[...].astype(F32)
    u_prev_row = jnp.where(first, 0.0, u_halo_prev[halo - 1:halo, :])
    u_next_row = jnp.where(final, 0.0, u_halo_next[0:1, :])
    ridx = lax.broadcasted_iota(jnp.int32, u.shape, 0)
    u_prev = jnp.where(ridx == 0, u_prev_row, pltpu.roll(u, 1, axis=0))
    u_next = jnp.where(ridx == tm - 1, u_next_row, pltpu.roll(u, tm - 1, axis=0))
    cw = cw_ref[...]
    cv = cb_ref[...].astype(F32) * (cw[0:1, :] * u_prev + cw[1:2, :] * u + cw[2:3, :] * u_next)

    att = jnp.where(in_p, attp_ref[...], atts_ref[...])
    acc = jax.nn.sigmoid(gl0_ref[...].astype(F32)) * jnp.dot(att, wb_ref[0],
                                                             preferred_element_type=F32)
    acc += jax.nn.sigmoid(gl1_ref[...].astype(F32)) * jnp.dot(ml.astype(BF16), wb_ref[1],
                                                              preferred_element_type=F32)
    acc += jax.nn.sigmoid(gl2_ref[...].astype(F32)) * jnp.dot(cv.astype(BF16), wb_ref[2],
                                                              preferred_element_type=F32)
    o_ref[...] = acc.astype(BF16)


def _merge(att_p, att_s, hdir_p, hdir_s, p16, wb, mnorm, convw):
    tm = M_CHUNK
    halo = 16
    nt = N_TOK // tm
    npt = N_P // tm
    th = tm // halo
    last_h = N_TOK // halo - 1
    p_idx = lambda i: jnp.minimum(i, npt - 1)
    s_idx = lambda i: jnp.maximum(i - npt, 0)
    cblk = lambda col: (lambda i: (i, col // BRANCH_W))
    prevh = lambda col: (lambda i: (jnp.maximum(i * th - 1, 0), col // BRANCH_W))
    nexth = lambda col: (lambda i: (jnp.minimum((i + 1) * th, last_h), col // BRANCH_W))
    glblk = lambda g: (lambda i: (i, C16_GL // D_MODEL + g))
    hdir_p = hdir_p.reshape(2, N_P, M_WIDTH)
    hdir_s = hdir_s.reshape(2, N_S, M_WIDTH)
    return pl.pallas_call(
        functools.partial(_merge_kernel, tm=tm, halo=halo),
        grid=(nt,),
        in_specs=[
            pl.BlockSpec((tm, BRANCH_W), lambda i: (p_idx(i), 0)),
            pl.BlockSpec((tm, BRANCH_W), lambda i: (s_idx(i), 0)),
            pl.BlockSpec((None, tm, BRANCH_W), lambda i: (0, p_idx(i), 0)),
            pl.BlockSpec((None, tm, BRANCH_W), lambda i: (1, p_idx(i), 0)),
            pl.BlockSpec((None, tm, BRANCH_W), lambda i: (0, s_idx(i), 0)),
            pl.BlockSpec((None, tm, BRANCH_W), lambda i: (1, s_idx(i), 0)),
            pl.BlockSpec((tm, BRANCH_W), cblk(C16_MO)),
            pl.BlockSpec((tm, BRANCH_W), cblk(C16_CB)),
            pl.BlockSpec((tm, BRANCH_W), cblk(C16_CC)),
            pl.BlockSpec((tm, BRANCH_W), cblk(C16_CX)),
            pl.BlockSpec((halo, BRANCH_W), prevh(C16_CC)),
            pl.BlockSpec((halo, BRANCH_W), prevh(C16_CX)),
            pl.BlockSpec((halo, BRANCH_W), nexth(C16_CC)),
            pl.BlockSpec((halo, BRANCH_W), nexth(C16_CX)),
            pl.BlockSpec((tm, D_MODEL), glblk(0)),
            pl.BlockSpec((tm, D_MODEL), glblk(1)),
            pl.BlockSpec((tm, D_MODEL), glblk(2)),
            pl.BlockSpec((N_BRANCH, BRANCH_W, D_MODEL), lambda i: (0, 0, 0)),
            pl.BlockSpec((1, M_WIDTH), lambda i: (0, 0)),
            pl.BlockSpec((8, CONV_WIDTH), lambda i: (0, 0)),
        ],
        out_specs=pl.BlockSpec((tm, D_MODEL), lambda i: (i, 0)),
        out_shape=jax.ShapeDtypeStruct((N_TOK, D_MODEL), BF16),
        compiler_params=_cparams(("arbitrary",), 48),
        name="branch_merge",
    )(att_p, att_s, hdir_p, hdir_p, hdir_s, hdir_s, p16, p16, p16, p16, p16, p16, p16, p16,
      p16, p16, p16, wb, mnorm, convw)


def _outproj_kernel(m_ref, w_ref, x_ref, gpost_ref, gate_ref, gpre_ref, sc_ref, sh_ref,
                    x_out_ref, h_out_ref):
    mix = jnp.dot(m_ref[...], w_ref[...], preferred_element_type=F32)
    x = x_ref[...] + gate_ref[...] * (_rms(mix) * gpost_ref[...])
    x_out_ref[...] = x
    h = (_rms(x) * gpre_ref[...]) * (1 + sc_ref[...]) + sh_ref[...]
    h_out_ref[...] = h.astype(BF16)


def _outproj(merged, w_out, x, gpost, gate, gpre, sc, sh):
    tm = 256
    row = lambda i: (_cond_row(i, tm), 0, 0)
    vec = pl.BlockSpec((1, D_MODEL), lambda i: (0, 0))
    cond = pl.BlockSpec((None, 1, D_MODEL), row)
    tile = pl.BlockSpec((tm, D_MODEL), lambda i: (i, 0))
    return pl.pallas_call(
        _outproj_kernel,
        grid=(N_TOK // tm,),
        in_specs=[tile, pl.BlockSpec((D_MODEL, D_MODEL), lambda i: (0, 0)), tile,
                  vec, cond, vec, cond, cond],
        out_specs=[tile, tile],
        out_shape=[jax.ShapeDtypeStruct((N_TOK, D_MODEL), F32),
                   jax.ShapeDtypeStruct((N_TOK, D_MODEL), BF16)],
        compiler_params=_cparams(("parallel",), 48),
        name="out_projection",
    )(merged, w_out, x, gpost, gate, gpre, sc, sh)


def _ffn_kernel(*refs, emit_next, n_p_tiles):
    h_ref, wg_ref, wu_ref, wo_ref, x_ref, gpost_ref, gate_ref = refs[:7]
    if emit_next:
        gpre_ref, sc_ref, sh_ref, x_out_ref, h_out_ref, acc_ref = refs[7:]
    else:
        yp_ref, ys_ref, acc_ref = refs[7:]

    i = pl.program_id(0)
    j = pl.program_id(1)

    @pl.when(j == 0)
    def _():
        acc_ref[...] = jnp.zeros_like(acc_ref)

    h = h_ref[...]
    gate = jnp.dot(h, wg_ref[...], preferred_element_type=F32)
    up = jnp.dot(h, wu_ref[...], preferred_element_type=F32)
    act = (gate * jax.nn.sigmoid(gate) * up).astype(BF16)
    acc_ref[...] += jnp.dot(act, wo_ref[...], preferred_element_type=F32)

    def result():
        return x_ref[...] + gate_ref[...] * (_rms(acc_ref[...]) * gpost_ref[...])

    is_last = j == pl.num_programs(1) - 1
    if emit_next:
        @pl.when(is_last)
        def _():
            x = result()
            x_out_ref[...] = x
            hn = (_rms(x) * gpre_ref[...]) * (1 + sc_ref[...]) + sh_ref[...]
            h_out_ref[...] = hn.astype(BF16)
    else:
        @pl.when(is_last & (i < n_p_tiles))
        def _():
            yp_ref[...] = result()

        @pl.when(is_last & (i >= n_p_tiles))
        def _():
            ys_ref[...] = result()


def _ffn(h, w_in, w_out, x, gpost, gate, nxt=None):
    tm, th = 512, 512
    nj = FF_HIDDEN // th
    npt = N_P // tm
    row = lambda i, j: (_cond_row(i, tm), 0, 0)
    vec = pl.BlockSpec((1, D_MODEL), lambda i, j: (0, 0))
    cond = pl.BlockSpec((None, 1, D_MODEL), row)
    tile = pl.BlockSpec((tm, D_MODEL), lambda i, j: (i, 0))
    in_specs = [
        tile,
        pl.BlockSpec((D_MODEL, th), lambda i, j: (0, j)),
        pl.BlockSpec((D_MODEL, th), lambda i, j: (0, nj + j)),
        pl.BlockSpec((th, D_MODEL), lambda i, j: (j, 0)),
        tile, vec, cond,
    ]
    args = [h, w_in, w_in, w_out, x, gpost, gate]
    if nxt is not None:
        in_specs += [vec, cond, cond]
        args += list(nxt)
        out_specs = [tile, tile]
        out_shape = [jax.ShapeDtypeStruct((N_TOK, D_MODEL), F32),
                     jax.ShapeDtypeStruct((N_TOK, D_MODEL), BF16)]
    else:
        out_specs = [
            pl.BlockSpec((tm, D_MODEL), lambda i, j: (jnp.minimum(i, npt - 1), 0)),
            pl.BlockSpec((tm, D_MODEL), lambda i, j: (jnp.maximum(i - npt, 0), 0)),
        ]
        out_shape = [jax.ShapeDtypeStruct((N_P, D_MODEL), F32),
                     jax.ShapeDtypeStruct((N_S, D_MODEL), F32)]
    return pl.pallas_call(
        functools.partial(_ffn_kernel, emit_next=nxt is not None, n_p_tiles=npt),
        grid=(N_TOK // tm, nj),
        in_specs=in_specs,
        out_specs=out_specs,
        out_shape=out_shape,
        scratch_shapes=[pltpu.VMEM((tm, D_MODEL), F32)],
        compiler_params=_cparams(("arbitrary", "arbitrary"), 56),
        name="ffn",
    )(*args)


def kernel(x_prompt, x_sample, cache_k, cache_v, state_C, state_n, state_m, c, c_ctx, w_mod, b_mod, norm_pre1, norm_post1, norm_pre2, norm_post2, w_in, q_norm, k_norm, mlstm_gate_bias, mlstm_norm, conv_w, w_branch, w_out, w_ffn_in, w_ffn_out):
    cond = jnp.concatenate([c_ctx[None], c, jnp.zeros((N_COND - 1 - DEC_BATCH, D_MODEL), F32)], axis=0)
    mod = _modulation(cond, w_mod, b_mod)
    mod = mod.reshape(DEPTH, N_COND, 6, 1, D_MODEL).transpose(0, 2, 1, 3, 4)
    vec = lambda a: a.reshape(1, -1)
    cos, sin = _rope_tables()

    x, h = _prenorm(x_prompt.reshape(N_P, D_MODEL), x_sample.reshape(N_S, D_MODEL),
                    vec(norm_pre1[0]), mod[0, 1], mod[0, 0])

    ck = cache_k.reshape(DEC_BATCH, DEPTH, PAST_LEN, KV_WIDTH)
    cv = cache_v.reshape(DEC_BATCH, DEPTH, PAST_LEN, KV_WIDTH)
    st_n = state_n.reshape(DEC_BATCH, DEPTH, 2, M_HEADS, 1, M_DK)
    w_branch16 = w_branch.astype(BF16)
    w_out16 = w_out.astype(BF16)
    w_ffn_in16 = w_ffn_in.astype(BF16)
    w_ffn_out16 = w_ffn_out.astype(BF16)

    cache, state = None, None
    for l in range(DEPTH):
        sh1, sc1, g1, sh2, sc2, g2 = (mod[l, i] for i in range(6))
        gate_bias = jnp.pad(mlstm_gate_bias[l], (0, LANES - GATE_COLS)).reshape(1, LANES)

        p32, p16 = _projection(h, w_in, l)
        col, row = _gates(h, w_in, l, gate_bias)

        qw, kw = vec(q_norm[l]), vec(k_norm[l])
        att_p, new_k, new_v = _attention_ctx(p32, qw, kw, l, cache)
        cache = (new_k, new_v)
        att_s = _attention_dec(p32, ck, cv, qw, kw, cos, sin, l)

        hdir_p, *state = _mlstm(p16, col, row, n_seq=BATCH, seq_len=SEQ, row0=0, layer=l,
                                prev_state=state, emit_state=True)
        m0 = jnp.broadcast_to(state_m[:, l].reshape(DEC_BATCH, 2, M_HEADS, 1, 1),
                              (DEC_BATCH, 2, M_HEADS, 1, LANES))
        (hdir_s,) = _mlstm(p16, col, row, n_seq=DEC_BATCH, seq_len=DEC_SEQ, row0=N_P, layer=l,
                           init=(state_C, st_n, m0))

        merged = _merge(att_p, att_s, hdir_p, hdir_s, p16, w_branch16[l], vec(mlstm_norm[l]),
                        jnp.pad(conv_w[l], ((0, 5), (0, 0))))
        x, h2 = _outproj(merged, w_out16[l], x, vec(norm_post1[l]), g1,
                         vec(norm_pre2[l]), sc2, sh2)
        if l + 1 < DEPTH:
            nxt = (vec(norm_pre1[l + 1]), mod[l + 1, 1], mod[l + 1, 0])
            x, h = _ffn(h2, w_ffn_in16[l], w_ffn_out16[l], x, vec(norm_post2[l]), g2, nxt)
        else:
            y_p, y_s = _ffn(h2, w_ffn_in16[l], w_ffn_out16[l], x, vec(norm_post2[l]), g2)

    new_k, new_v = cache
    c_fin, n_fin, m_fin = state
    return (y_p.reshape(BATCH, SEQ, D_MODEL), y_s.reshape(DEC_BATCH, DEC_SEQ, D_MODEL),
            new_k.reshape(BATCH, DEPTH, SEQ, ATT_KV_HEADS, HEAD_DIM),
            new_v.reshape(BATCH, DEPTH, SEQ, ATT_KV_HEADS, HEAD_DIM),
            c_fin, n_fin.reshape(BATCH, DEPTH, 2, M_HEADS, M_DK), m_fin[:, :, :, :, 0, 0])
```

```python
import functools

import jax
import jax.numpy as jnp
from jax import lax
from jax.experimental import pallas as pl
from jax.experimental.pallas import tpu as pltpu

F32 = jnp.float32
BF16 = jnp.bfloat16

D_MODEL = 2048
BATCH = 16
SEQ = 256
DEPTH = 2
DEC_BATCH = 2
DEC_SEQ = 2048
PAST_LEN = 256
GRID_W = 64
EPS = 1e-6
HEAD_DIM = 128
ATT_Q_HEADS = 8
ATT_KV_HEADS = 2
ATT_GROUP = ATT_Q_HEADS // ATT_KV_HEADS
ATT_WIDTH = ATT_Q_HEADS * HEAD_DIM
KV_WIDTH = ATT_KV_HEADS * HEAD_DIM
ROPE_THETA = 10000.0
ROPE_FREQS = HEAD_DIM // 4
M_HEADS = 4
M_DK = 256
M_DV = 256
M_WIDTH = M_HEADS * M_DV
M_CHUNK = 128
CONV_WIDTH = 1024
N_BRANCH = 3
BRANCH_W = 1024
FF_HIDDEN = 5632
IN_WIDTH = 14864

LANES = 128
N_P = BATCH * SEQ
N_S = DEC_BATCH * DEC_SEQ
N_TOK = N_P + N_S
N_CHUNKS = N_TOK // M_CHUNK
N_COND = 8
KV_LEN_S = PAST_LEN + DEC_SEQ

W_AQ, W_AK, W_AV = 0, 1024, 1280
W_MQ, W_MV, W_MG, W_CB = 1536, 3584, 5632, 5648
GATE_COLS = 4 * M_HEADS
PROJ_TN = 512
N_TILE_A = W_MG // PROJ_TN
N_TILE_C = (IN_WIDTH - W_CB) // PROJ_TN
N_TILE_32 = W_MQ // PROJ_TN
TILE_T0 = W_MV // PROJ_TN
TILE_T1 = N_TILE_A

P32_W = W_MQ
C16_GL, C16_MQ, C16_MK = 0, 6144, 7168
C16_CB, C16_CC, C16_CX = 8192, 9216, 10240
P16_W = 11264
RT_MV, RT_MO = 0, 1024
PT_H = 2048
M_AUG = M_DV + LANES
GATE_ROWS = 24

NT_DIMS = (((1,), (1,)), ((), ()))
TN_DIMS = (((0,), (0,)), ((), ()))


def _cparams(semantics, vmem_mb):
    return pltpu.CompilerParams(dimension_semantics=semantics,
                                vmem_limit_bytes=vmem_mb * 1024 * 1024)


def _rms(x):
    return x * lax.rsqrt(jnp.mean(x * x, axis=-1, keepdims=True) + EPS)


def _cond_row(tile, tm):
    return jnp.where(tile < N_P // tm, 0, 1 + (tile * tm - N_P) // DEC_SEQ)


def _mod_kernel(c_ref, w_ref, b_ref, o_ref):
    c = c_ref[...]
    a = (c * jax.nn.sigmoid(c)).astype(BF16)
    o_ref[...] = jnp.dot(a, w_ref[...].astype(BF16), preferred_element_type=F32) + b_ref[...]


def _modulation(cond, w_mod, b_mod):
    tn = 1024
    n = 6 * D_MODEL
    return pl.pallas_call(
        _mod_kernel,
        grid=(DEPTH, n // tn),
        in_specs=[
            pl.BlockSpec((N_COND, D_MODEL), lambda l, j: (0, 0)),
            pl.BlockSpec((None, D_MODEL, tn), lambda l, j: (l, 0, j)),
            pl.BlockSpec((None, 1, tn), lambda l, j: (l, 0, j)),
        ],
        out_specs=pl.BlockSpec((None, N_COND, tn), lambda l, j: (l, 0, j)),
        out_shape=jax.ShapeDtypeStruct((DEPTH, N_COND, n), F32),
        compiler_params=_cparams(("parallel", "parallel"), 32),
        name="modulation",
    )(cond, w_mod, b_mod.reshape(DEPTH, 1, n))


def _prenorm_kernel(xp_ref, xs_ref, g_ref, sc_ref, sh_ref, x_ref, h_ref, *, n_p_tiles):
    def emit(src_ref):
        x = src_ref[...]
        x_ref[...] = x
        y = _rms(x) * g_ref[...]
        h_ref[...] = (y * (1 + sc_ref[...]) + sh_ref[...]).astype(BF16)

    i = pl.program_id(0)
    pl.when(i < n_p_tiles)(lambda: emit(xp_ref))
    pl.when(i >= n_p_tiles)(lambda: emit(xs_ref))


def _prenorm(xp, xs, g, sc, sh):
    tm = 512
    npt = N_P // tm
    row = lambda i: (_cond_row(i, tm), 0, 0)
    tile = pl.BlockSpec((tm, D_MODEL), lambda i: (i, 0))
    return pl.pallas_call(
        functools.partial(_prenorm_kernel, n_p_tiles=npt),
        grid=(N_TOK // tm,),
        in_specs=[
            pl.BlockSpec((tm, D_MODEL), lambda i: (jnp.minimum(i, npt - 1), 0)),
            pl.BlockSpec((tm, D_MODEL), lambda i: (jnp.maximum(i - npt, 0), 0)),
            pl.BlockSpec((1, D_MODEL), lambda i: (0, 0)),
            pl.BlockSpec((None, 1, D_MODEL), row),
            pl.BlockSpec((None, 1, D_MODEL), row),
        ],
        out_specs=[tile, tile],
        out_shape=[jax.ShapeDtypeStruct((N_TOK, D_MODEL), F32),
                   jax.ShapeDtypeStruct((N_TOK, D_MODEL), BF16)],
        compiler_params=_cparams(("arbitrary",), 32),
        name="prenorm",
    )(xp, xs, g, sc, sh)


def _proj_tile16(j):
    first_gl = N_TILE_A + 3 * CONV_WIDTH // PROJ_TN
    mq0 = C16_MQ // PROJ_TN
    return jnp.where(j < TILE_T0, mq0 + jnp.maximum(j - N_TILE_32, 0),
                     jnp.where(j < TILE_T1, mq0 + TILE_T0 - N_TILE_32 - 1,
                               jnp.where(j < first_gl, C16_CB // PROJ_TN + (j - N_TILE_A),
                                         j - first_gl)))


def _proj_kernel(x_ref, w_ref, tail_ref, o32_ref, o16_ref, ot_ref, w_scr):
    j = pl.program_id(1)
    shifted = j >= N_TILE_A
    off = pl.multiple_of(jnp.where(shifted, GATE_COLS, 0), 8)
    body = PROJ_TN - GATE_COLS
    w_scr[0:body, :] = w_ref[pl.ds(off, body), :].astype(BF16)
    w_scr[body:, :] = jnp.where(shifted, tail_ref[...], w_ref[body:, :]).astype(BF16)

    transposed = (j >= TILE_T0) & (j < TILE_T1)

    @pl.when(transposed)
    def _():
        acc_t = lax.dot_general(w_scr[...], x_ref[...], NT_DIMS, preferred_element_type=F32)
        ot_ref[...] = acc_t.astype(BF16)

    @pl.when(jnp.logical_not(transposed))
    def _():
        acc = lax.dot_general(x_ref[...], w_scr[...], NT_DIMS, preferred_element_type=F32)

        @pl.when(j < N_TILE_32)
        def _():
            o32_ref[...] = acc

        @pl.when(j >= N_TILE_32)
        def _():
            o16_ref[...] = acc.astype(BF16)


def _projection(h, w_t, layer):
    tm = 2048
    nj = N_TILE_A + N_TILE_C
    tails_per_tile = PROJ_TN // GATE_COLS
    tail0 = W_MG // GATE_COLS

    def tail_idx(i, j):
        return (layer, jnp.where(j < N_TILE_A, tail0, (j + 1) * tails_per_tile), 0)

    return pl.pallas_call(
        _proj_kernel,
        grid=(N_TOK // tm, nj),
        in_specs=[
            pl.BlockSpec((tm, D_MODEL), lambda i, j: (i, 0), pipeline_mode=pl.Buffered(1)),
            pl.BlockSpec((None, PROJ_TN, D_MODEL), lambda i, j: (layer, j, 0)),
            pl.BlockSpec((None, GATE_COLS, D_MODEL), tail_idx),
        ],
        out_specs=[
            pl.BlockSpec((tm, PROJ_TN), lambda i, j: (i, jnp.minimum(j, N_TILE_32 - 1))),
            pl.BlockSpec((tm, PROJ_TN), lambda i, j: (i, _proj_tile16(j))),
            pl.BlockSpec((PROJ_TN, tm),
                         lambda i, j: (jnp.clip(j - TILE_T0, 0, TILE_T1 - TILE_T0 - 1), i)),
        ],
        out_shape=[jax.ShapeDtypeStruct((N_TOK, P32_W), F32),
                   jax.ShapeDtypeStruct((N_TOK, P16_W), BF16),
                   jax.ShapeDtypeStruct((PT_H, N_TOK), BF16)],
        scratch_shapes=[pltpu.VMEM((PROJ_TN, D_MODEL), BF16)],
        compiler_params=_cparams(("arbitrary", "arbitrary"), 52),
        name="projection",
    )(h, w_t, w_t)


def _gate_kernel(h_ref, w_ref, b_ref, ld_ref, row_ref, *, chunks):
    g = lax.dot_general(h_ref[...], w_ref[...].astype(BF16), NT_DIMS,
                        preferred_element_type=F32) + b_ref[...]
    lf = jax.nn.log_sigmoid(g)
    s_idx = lax.broadcasted_iota(jnp.int32, (M_CHUNK, M_CHUNK), 0)
    t_idx = lax.broadcasted_iota(jnp.int32, (M_CHUNK, M_CHUNK), 1)
    tril = (t_idx <= s_idx).astype(F32)
    triu = (t_idx >= s_idx).astype(F32)
    ones = jnp.ones((M_CHUNK, M_CHUNK), F32)
    lane = lax.broadcasted_iota(jnp.int32, (M_CHUNK, LANES), 1)
    exact = dict(preferred_element_type=F32, precision=lax.Precision.HIGHEST)
    for ch in range(chunks):
        rows = slice(ch * M_CHUNK, (ch + 1) * M_CHUNK)
        gc = g[rows]
        lfc = lf[rows]
        pre = jnp.dot(tril, lfc, **exact)
        suf = jnp.dot(triu, lfc, **exact)
        tot = jnp.dot(ones, lfc, **exact)
        both = jnp.where((lane >= 4) & (lane < 8), pre,
                         jnp.where((lane >= 12) & (lane < 16), suf, gc))
        both_t = both.T
        tot_t = tot.T
        for d in range(2):
            mask = (s_idx <= t_idx) if d == 0 else (s_idx >= t_idx)
            row_ref[d, ch, 20:GATE_ROWS, :] = jnp.zeros((GATE_ROWS - 20, M_CHUNK), F32)
            for hd in range(M_HEADS):
                li, lb = 8 * d + hd, 8 * d + 4 + hd
                i_row = both_t[li:li + 1, :]
                b_row = both_t[lb:lb + 1, :]
                bl_row = tot_t[lb:lb + 1, :]
                key = both[:, lb:lb + 1] - both[:, li:li + 1]
                ld = jnp.where(mask, b_row - key, -jnp.inf)
                wl = bl_row - b_row + i_row
                ld_ref[d, hd, rows, :] = ld
                row_ref[d, ch, hd:hd + 1, :] = i_row
                row_ref[d, ch, 4 + hd:5 + hd, :] = b_row
                row_ref[d, ch, 8 + hd:9 + hd, :] = jnp.max(ld, axis=0, keepdims=True)
                row_ref[d, ch, 12 + hd:13 + hd, :] = bl_row
                row_ref[d, ch, 16 + hd:17 + hd, :] = jnp.broadcast_to(
                    jnp.max(wl, axis=-1, keepdims=True), (1, M_CHUNK))


def _gates(h, w_t, layer, bias):
    tm = 512
    chunks = tm // M_CHUNK
    return pl.pallas_call(
        functools.partial(_gate_kernel, chunks=chunks),
        grid=(N_TOK // tm,),
        in_specs=[
            pl.BlockSpec((tm, D_MODEL), lambda i: (i, 0)),
            pl.BlockSpec((None, LANES, D_MODEL), lambda i: (layer, W_MG // LANES, 0)),
            pl.BlockSpec((1, LANES), lambda i: (0, 0)),
        ],
        out_specs=[
            pl.BlockSpec((2, M_HEADS, tm, M_CHUNK), lambda i: (0, 0, i, 0)),
            pl.BlockSpec((2, chunks, GATE_ROWS, M_CHUNK), lambda i: (0, i, 0, 0)),
        ],
        out_shape=[
            jax.ShapeDtypeStruct((2, M_HEADS, N_TOK, M_CHUNK), F32),
            jax.ShapeDtypeStruct((2, N_CHUNKS, GATE_ROWS, M_CHUNK), F32),
        ],
        compiler_params=_cparams(("parallel",), 32),
        name="mlstm_gates",
    )(h, w_t, bias)


def _mlstm_kernel(*refs, n_par, has_init, emit_state, has_prev):
    pos = 0
    seq_refs = []
    for _ in range(n_par):
        seq_refs.append(refs[pos:pos + 5])
        pos += 5
    if has_init:
        c0_ref, n0_ref, m0_ref = refs[pos:pos + 3]
        pos += 3
    if has_prev:
        pos += 3
    h_ref = refs[pos]
    pos += 1
    if emit_state:
        cout_ref, nout_ref, mout_ref = refs[pos:pos + 3]
        pos += 3
    c_scr, m_scr = refs[pos:pos + 2]

    ci = pl.program_id(2)
    last = pl.num_programs(2) - 1
    pad_rows = lax.broadcasted_iota(jnp.int32, (LANES, M_CHUNK), 0)
    one_row = (pad_rows == 0).astype(F32)

    @pl.when(ci == 0)
    def _():
        if has_init:
            for u in range(n_par):
                for hd in range(M_HEADS):
                    c_scr[u, hd, 0:M_DV, :] = c0_ref[u, hd].T
                    pad = lax.broadcasted_iota(jnp.int32, (LANES, M_DK), 0)
                    c_scr[u, hd, M_DV:M_AUG, :] = jnp.where(pad == 0, n0_ref[u, hd], 0.0)
            m_scr[...] = m0_ref[...]
        else:
            c_scr[...] = jnp.zeros_like(c_scr)
            m_scr[...] = jnp.zeros_like(m_scr)

    for u in range(n_par):
        q_ref, k_ref, vt_ref, ld_ref, row_ref = seq_refs[u]
        for hd in range(M_HEADS):
            cols = slice(hd * M_DK, (hd + 1) * M_DK)
            q = q_ref[:, cols]
            k = k_ref[:, cols] * (M_DK ** -0.5)
            vt_aug = jnp.concatenate([vt_ref[cols, :].astype(F32), one_row], axis=0)
            ld = ld_ref[hd]
            i_row = row_ref[hd:hd + 1, :]
            b_row = row_ref[4 + hd:5 + hd, :]
            ldmax_row = row_ref[8 + hd:9 + hd, :]
            bl_row = row_ref[12 + hd:13 + hd, :]
            wmax_row = row_ref[16 + hd:17 + hd, :]
            mem = c_scr[u, hd]
            m_prev = m_scr[u, hd]

            g_row = b_row + m_prev
            mt_row = jnp.maximum(g_row, ldmax_row)
            kq = lax.dot_general(k, q, NT_DIMS, preferred_element_type=F32)
            st = (kq * jnp.exp(ld - mt_row)).astype(BF16)
            nd = jnp.exp(g_row - mt_row) * lax.dot_general(
                mem.astype(BF16), q, NT_DIMS, preferred_element_type=F32) \
                + jnp.dot(vt_aug.astype(BF16), st, preferred_element_type=F32)
            den_row = nd[M_DV:M_DV + 1, :]
            scale_row = 1.0 / jnp.maximum(jnp.abs(den_row), jnp.exp(-mt_row))
            h_ref[u, cols, :] = nd[0:M_DV, :] * scale_row

            m_new = jnp.maximum(bl_row + m_prev, wmax_row)
            w_row = jnp.exp(bl_row - b_row + i_row - m_new)
            dec = jnp.exp(bl_row + m_prev - m_new)
            dec_wide = jnp.concatenate([dec] * (M_DK // LANES), axis=1)
            c_scr[u, hd] = dec_wide * mem + jnp.dot((vt_aug * w_row).astype(BF16), k,
                                                    preferred_element_type=F32)
            m_scr[u, hd] = m_new

    if emit_state:
        @pl.when(ci == last)
        def _():
            for u in range(n_par):
                for hd in range(M_HEADS):
                    cout_ref[u, hd] = c_scr[u, hd, 0:M_DV, :].T
                    nout_ref[u, hd] = c_scr[u, hd, M_DV:M_DV + 1, :]
            mout_ref[...] = m_scr[...]


def _mlstm(p16, pt16, ld, rows, *, n_seq, seq_len, row0, layer, init=None, prev_state=None,
           emit_state=False):
    n_par = 2
    nc = seq_len // M_CHUNK
    blk0 = row0 // M_CHUNK

    def chunk_of(c, d):
        return jnp.where(d == 0, c, nc - 1 - c)

    in_specs, args = [], []
    for u in range(n_par):
        blk = lambda g, d, c, u=u: blk0 + (g * n_par + u) * nc + chunk_of(c, d)
        in_specs += [
            pl.BlockSpec((M_CHUNK, M_WIDTH), lambda g, d, c, blk=blk: (blk(g, d, c), C16_MQ // M_WIDTH)),
            pl.BlockSpec((M_CHUNK, M_WIDTH), lambda g, d, c, blk=blk: (blk(g, d, c), C16_MK // M_WIDTH)),
            pl.BlockSpec((M_WIDTH, M_CHUNK), lambda g, d, c, blk=blk: (RT_MV // M_WIDTH, blk(g, d, c))),
            pl.BlockSpec((None, M_HEADS, M_CHUNK, M_CHUNK), lambda g, d, c, blk=blk: (d, 0, blk(g, d, c), 0)),
            pl.BlockSpec((None, None, GATE_ROWS, M_CHUNK), lambda g, d, c, blk=blk: (d, blk(g, d, c), 0, 0)),
        ]
        args += [p16, p16, pt16, ld, rows]
    if init is not None:
        in_specs += [
            pl.BlockSpec((n_par, None, None, M_HEADS, M_DK, M_DV), lambda g, d, c: (g, layer, d, 0, 0, 0)),
            pl.BlockSpec((n_par, None, None, M_HEADS, 1, M_DK), lambda g, d, c: (g, layer, d, 0, 0, 0)),
            pl.BlockSpec((n_par, None, M_HEADS, 1, LANES), lambda g, d, c: (g, d, 0, 0, 0)),
        ]
        args += list(init)
    aliases = {}
    if prev_state is not None:
        for k_, a in enumerate(prev_state):
            aliases[len(args)] = 1 + k_
            in_specs.append(pl.BlockSpec(memory_space=pl.ANY))
            args.append(a)
    out_specs = [pl.BlockSpec((None, n_par, None, M_WIDTH, M_CHUNK),
                              lambda g, d, c: (d, g, chunk_of(c, d), 0, 0))]
    out_shape = [jax.ShapeDtypeStruct((2, n_seq, nc, M_WIDTH, M_CHUNK), F32)]
    if emit_state:
        out_specs += [
            pl.BlockSpec((n_par, None, None, M_HEADS, M_DK, M_DV), lambda g, d, c: (g, layer, d, 0, 0, 0)),
            pl.BlockSpec((n_par, None, None, M_HEADS, 1, M_DK), lambda g, d, c: (g, layer, d, 0, 0, 0)),
            pl.BlockSpec((n_par, None, None, M_HEADS, 1, LANES), lambda g, d, c: (g, layer, d, 0, 0, 0)),
        ]
        out_shape += [
            jax.ShapeDtypeStruct((n_seq, DEPTH, 2, M_HEADS, M_DK, M_DV), F32),
            jax.ShapeDtypeStruct((n_seq, DEPTH, 2, M_HEADS, 1, M_DK), F32),
            jax.ShapeDtypeStruct((n_seq, DEPTH, 2, M_HEADS, 1, LANES), F32),
        ]
    return pl.pallas_call(
        functools.partial(_mlstm_kernel, n_par=n_par, has_init=init is not None,
                          emit_state=emit_state, has_prev=prev_state is not None),
        grid=(n_seq // n_par, 2, nc),
        in_specs=in_specs,
        out_specs=out_specs,
        out_shape=out_shape,
        input_output_aliases=aliases,
        scratch_shapes=[
            pltpu.VMEM((n_par, M_HEADS, M_AUG, M_DK), F32),
            pltpu.VMEM((n_par, M_HEADS, 1, LANES), F32),
        ],
        compiler_params=_cparams(("arbitrary", "arbitrary", "arbitrary"), 40),
        name="mlstm_scan_dec" if init is not None else "mlstm_scan_ctx",
    )(*args)


def _swap32(x):
    lane = lax.broadcasted_iota(jnp.int32, x.shape, 1)
    return jnp.where((lane % 64) < 32, pltpu.roll(x, HEAD_DIM - 32, axis=1),
                     pltpu.roll(x, 32, axis=1))


def _softmax_pv(q, kb, vb):
    s = lax.dot_general(q, kb, NT_DIMS, preferred_element_type=F32)
    e = jnp.exp(s - jnp.max(s, axis=-1, keepdims=True))
    l = jnp.sum(e, axis=-1, keepdims=True)
    return jnp.dot(e.astype(BF16), vb, preferred_element_type=F32) / l


def _attn_ctx_kernel(*refs):
    q_ref, k_ref, v_ref, qw_ref, kw_ref = refs[:5]
    o_ref, kout_ref, vout_ref = refs[-3:]
    k = _rms(k_ref[...]) * kw_ref[...]
    v = v_ref[...]
    kout_ref[...] = k
    vout_ref[...] = v
    kb = k.astype(BF16)
    vb = v.astype(BF16)
    for g in range(ATT_GROUP):
        cols = slice(g * HEAD_DIM, (g + 1) * HEAD_DIM)
        q = _rms(q_ref[:, cols]) * qw_ref[...] * (HEAD_DIM ** -0.5)
        o_ref[:, cols] = _softmax_pv(q.astype(BF16), kb, vb).astype(BF16)


def _attention_ctx(p32, qw, kw, layer, prev_cache=None):
    gw = ATT_GROUP * HEAD_DIM
    in_specs = [
        pl.BlockSpec((SEQ, gw), lambda b, h: (b, W_AQ // gw + h)),
        pl.BlockSpec((SEQ, HEAD_DIM), lambda b, h: (b, W_AK // HEAD_DIM + h)),
        pl.BlockSpec((SEQ, HEAD_DIM), lambda b, h: (b, W_AV // HEAD_DIM + h)),
        pl.BlockSpec((1, HEAD_DIM), lambda b, h: (0, 0)),
        pl.BlockSpec((1, HEAD_DIM), lambda b, h: (0, 0)),
    ]
    args = [p32, p32, p32, qw, kw]
    aliases = {}
    if prev_cache is not None:
        for k_, a in enumerate(prev_cache):
            aliases[len(args)] = 1 + k_
            in_specs.append(pl.BlockSpec(memory_space=pl.ANY))
            args.append(a)
    cache_spec = pl.BlockSpec((None, None, SEQ, HEAD_DIM), lambda b, h: (b, layer, 0, h))
    cache_shape = jax.ShapeDtypeStruct((BATCH, DEPTH, SEQ, KV_WIDTH), F32)
    return pl.pallas_call(
        _attn_ctx_kernel,
        grid=(BATCH, ATT_KV_HEADS),
        in_specs=in_specs,
        out_specs=[pl.BlockSpec((SEQ, gw), lambda b, h: (b, h)), cache_spec, cache_spec],
        out_shape=[jax.ShapeDtypeStruct((N_P, ATT_WIDTH), BF16), cache_shape, cache_shape],
        input_output_aliases=aliases,
        compiler_params=_cparams(("arbitrary", "arbitrary"), 32),
        name="attention_ctx",
    )(*args)


def _attn_dec_kernel(q_ref, k_ref, v_ref, ck_ref, cv_ref, qw_ref, kw_ref,
                     cos_ref, sin_ref, cosq_ref, sinq_ref, o_ref, k_scr, v_scr):
    @pl.when(pl.program_id(2) == 0)
    def _():
        k_scr[0:PAST_LEN, :] = ck_ref[...].astype(BF16)
        v_scr[0:PAST_LEN, :] = cv_ref[...].astype(BF16)
        k = _rms(k_ref[...]) * kw_ref[...]
        k = k * cos_ref[...] + _swap32(k) * sin_ref[...]
        k_scr[PAST_LEN:, :] = k.astype(BF16)
        v_scr[PAST_LEN:, :] = v_ref[...].astype(BF16)

    kb = k_scr[...]
    vb = v_scr[...]
    for g in range(ATT_GROUP):
        cols = slice(g * HEAD_DIM, (g + 1) * HEAD_DIM)
        q = _rms(q_ref[:, cols]) * qw_ref[...]
        q = (q * cosq_ref[...] + _swap32(q) * sinq_ref[...]) * (HEAD_DIM ** -0.5)
        o_ref[:, cols] = _softmax_pv(q.astype(BF16), kb, vb).astype(BF16)


def _attention_dec(p32, ck, cv, qw, kw, cos, sin, layer):
    tq = 128
    gw = ATT_GROUP * HEAD_DIM
    nq = DEC_SEQ // tq
    seq0 = N_P // DEC_SEQ
    cache_spec = pl.BlockSpec((None, None, PAST_LEN, HEAD_DIM), lambda b, h, i: (b, layer, 0, h))
    return pl.pallas_call(
        _attn_dec_kernel,
        grid=(DEC_BATCH, ATT_KV_HEADS, nq),
        in_specs=[
            pl.BlockSpec((tq, gw), lambda b, h, i: (N_P // tq + b * nq + i, W_AQ // gw + h)),
            pl.BlockSpec((DEC_SEQ, HEAD_DIM), lambda b, h, i: (seq0 + b, W_AK // HEAD_DIM + h)),
            pl.BlockSpec((DEC_SEQ, HEAD_DIM), lambda b, h, i: (seq0 + b, W_AV // HEAD_DIM + h)),
            cache_spec, cache_spec,
            pl.BlockSpec((1, HEAD_DIM), lambda b, h, i: (0, 0)),
            pl.BlockSpec((1, HEAD_DIM), lambda b, h, i: (0, 0)),
            pl.BlockSpec((DEC_SEQ, HEAD_DIM), lambda b, h, i: (0, 0)),
            pl.BlockSpec((DEC_SEQ, HEAD_DIM), lambda b, h, i: (0, 0)),
            pl.BlockSpec((tq, HEAD_DIM), lambda b, h, i: (i, 0)),
            pl.BlockSpec((tq, HEAD_DIM), lambda b, h, i: (i, 0)),
        ],
        out_specs=pl.BlockSpec((tq, gw), lambda b, h, i: (b * nq + i, h)),
        out_shape=jax.ShapeDtypeStruct((N_S, ATT_WIDTH), BF16),
        scratch_shapes=[
            pltpu.VMEM((KV_LEN_S, HEAD_DIM), BF16),
            pltpu.VMEM((KV_LEN_S, HEAD_DIM), BF16),
        ],
        compiler_params=_cparams(("arbitrary", "arbitrary", "arbitrary"), 32),
        name="attention_dec",
    )(p32, p32, p32, ck, cv, qw, kw, cos, sin, cos, sin)


def _rope_tables():
    t = jnp.arange(DEC_SEQ)
    inv = ROPE_THETA ** (-jnp.arange(ROPE_FREQS, dtype=F32) / ROPE_FREQS)
    ang = jnp.stack([t // GRID_W, t % GRID_W], axis=-1).astype(F32)[:, :, None] * inv
    cos = jnp.cos(ang)
    sin = jnp.sin(ang)
    cos = jnp.stack([cos, cos], axis=2).reshape(DEC_SEQ, HEAD_DIM)
    sin = jnp.stack([-sin, sin], axis=2).reshape(DEC_SEQ, HEAD_DIM)
    return cos, sin


def _merge_kernel(attp_ref, atts_ref, hfp_ref, hbp_ref, hfs_ref, hbs_ref,
                  mot_ref, cb_ref, cc_ref, cx_ref, ccp_ref, cxp_ref, ccn_ref, cxn_ref,
                  gl0_ref, gl1_ref, gl2_ref, wb_ref, mn_ref, cw_ref, o_ref, *, tm, halo):
    i = pl.program_id(0)
    row0 = i * tm
    in_p = row0 < N_P
    off = jnp.where(in_p, row0 % SEQ, (row0 - N_P) % DEC_SEQ)
    seq_len = jnp.where(in_p, SEQ, DEC_SEQ)
    first = off == 0
    final = off + tm == seq_len

    hsum = jnp.where(in_p, hfp_ref[...] + hbp_ref[...], hfs_ref[...] + hbs_ref[...])
    parts = []
    for hd in range(M_HEADS):
        x = hsum[hd * M_DV:(hd + 1) * M_DV, :]
        parts.append(x * lax.rsqrt(jnp.mean(x * x, axis=0, keepdims=True) + EPS))
    ml_t = jnp.concatenate(parts, axis=0) * mn_ref[...] * jax.nn.sigmoid(mot_ref[...].astype(F32))

    u = cc_ref[...].astype(F32) * cx_ref[...].astype(F32)
    u_halo_prev = ccp_ref[...].astype(F32) * cxp_ref[...].astype(F32)
    u_halo_next = ccn_ref[...].astype(F32) * cxn_ref[...].astype(F32)
    u_prev_row = jnp.where(first, 0.0, u_halo_prev[halo - 1:halo, :])
    u_next_row = jnp.where(final, 0.0, u_halo_next[0:1, :])
    ridx = lax.broadcasted_iota(jnp.int32, u.shape, 0)
    u_prev = jnp.where(ridx == 0, u_prev_row, pltpu.roll(u, 1, axis=0))
    u_next = jnp.where(ridx == tm - 1, u_next_row, pltpu.roll(u, tm - 1, axis=0))
    cw = cw_ref[...]
    cv = cb_ref[...].astype(F32) * (cw[0:1, :] * u_prev + cw[1:2, :] * u + cw[2:3, :] * u_next)

    att = jnp.where(in_p, attp_ref[...], atts_ref[...])
    acc = jax.nn.sigmoid(gl0_ref[...].astype(F32)) * jnp.dot(att, wb_ref[0],
                                                             preferred_element_type=F32)
    acc += jax.nn.sigmoid(gl1_ref[...].astype(F32)) * lax.dot_general(
        ml_t.astype(BF16), wb_ref[1], TN_DIMS, preferred_element_type=F32)
    acc += jax.nn.sigmoid(gl2_ref[...].astype(F32)) * jnp.dot(cv.astype(BF16), wb_ref[2],
                                                              preferred_element_type=F32)
    o_ref[...] = acc.astype(BF16)


def _merge(att_p, att_s, ht_p, ht_s, p16, pt16, wb, layer, mnorm_rep, convw):
    tm = M_CHUNK
    halo = 16
    nt = N_TOK // tm
    npt = N_P // tm
    th = tm // halo
    last_h = N_TOK // halo - 1
    p_idx = lambda i: jnp.minimum(i, npt - 1)
    s_idx = lambda i: jnp.maximum(i - npt, 0)
    cblk = lambda col: (lambda i: (i, col // BRANCH_W))
    prevh = lambda col: (lambda i: (jnp.maximum(i * th - 1, 0), col // BRANCH_W))
    nexth = lambda col: (lambda i: (jnp.minimum((i + 1) * th, last_h), col // BRANCH_W))
    glblk = lambda g: (lambda i: (i, C16_GL // D_MODEL + g))
    ht_p = ht_p.reshape(2, npt, M_WIDTH, M_CHUNK)
    ht_s = ht_s.reshape(2, nt - npt, M_WIDTH, M_CHUNK)
    ht_spec = lambda d, idx: pl.BlockSpec((None, None, M_WIDTH, M_CHUNK), lambda i: (d, idx(i), 0, 0))
    return pl.pallas_call(
        functools.partial(_merge_kernel, tm=tm, halo=halo),
        grid=(nt,),
        in_specs=[
            pl.BlockSpec((tm, BRANCH_W), lambda i: (p_idx(i), 0)),
            pl.BlockSpec((tm, BRANCH_W), lambda i: (s_idx(i), 0)),
            ht_spec(0, p_idx), ht_spec(1, p_idx), ht_spec(0, s_idx), ht_spec(1, s_idx),
            pl.BlockSpec((M_WIDTH, tm), lambda i: (RT_MO // M_WIDTH, i)),
            pl.BlockSpec((tm, BRANCH_W), cblk(C16_CB)),
            pl.BlockSpec((tm, BRANCH_W), cblk(C16_CC)),
            pl.BlockSpec((tm, BRANCH_W), cblk(C16_CX)),
            pl.BlockSpec((halo, BRANCH_W), prevh(C16_CC)),
            pl.BlockSpec((halo, BRANCH_W), prevh(C16_CX)),
            pl.BlockSpec((halo, BRANCH_W), nexth(C16_CC)),
            pl.BlockSpec((halo, BRANCH_W), nexth(C16_CX)),
            pl.BlockSpec((tm, D_MODEL), glblk(0)),
            pl.BlockSpec((tm, D_MODEL), glblk(1)),
            pl.BlockSpec((tm, D_MODEL), glblk(2)),
            pl.BlockSpec((None, N_BRANCH, BRANCH_W, D_MODEL), lambda i: (layer, 0, 0, 0)),
            pl.BlockSpec((M_WIDTH, LANES), lambda i: (0, 0)),
            pl.BlockSpec((8, CONV_WIDTH), lambda i: (0, 0)),
        ],
        out_specs=pl.BlockSpec((tm, D_MODEL), lambda i: (i, 0)),
        out_shape=jax.ShapeDtypeStruct((N_TOK, D_MODEL), BF16),
        compiler_params=_cparams(("arbitrary",), 48),
        name="branch_merge",
    )(att_p, att_s, ht_p, ht_p, ht_s, ht_s, pt16, p16, p16, p16, p16, p16, p16, p16,
      p16, p16, p16, wb, mnorm_rep, convw)


def _outproj_kernel(m_ref, w_ref, x_ref, gpost_ref, gate_ref, gpre_ref, sc_ref, sh_ref,
                    x_out_ref, h_out_ref):
    mix = jnp.dot(m_ref[...], w_ref[...], preferred_element_type=F32)
    x = x_ref[...] + gate_ref[...] * (_rms(mix) * gpost_ref[...])
    x_out_ref[...] = x
    h = (_rms(x) * gpre_ref[...]) * (1 + sc_ref[...]) + sh_ref[...]
    h_out_ref[...] = h.astype(BF16)


def _outproj(merged, w_out, layer, x, gpost, gate, gpre, sc, sh):
    tm = 256
    row = lambda i: (_cond_row(i, tm), 0, 0)
    vec = pl.BlockSpec((1, D_MODEL), lambda i: (0, 0))
    cond = pl.BlockSpec((None, 1, D_MODEL), row)
    tile = pl.BlockSpec((tm, D_MODEL), lambda i: (i, 0))
    return pl.pallas_call(
        _outproj_kernel,
        grid=(N_TOK // tm,),
        in_specs=[tile, pl.BlockSpec((None, D_MODEL, D_MODEL), lambda i: (layer, 0, 0)), tile,
                  vec, cond, vec, cond, cond],
        out_specs=[tile, tile],
        out_shape=[jax.ShapeDtypeStruct((N_TOK, D_MODEL), F32),
                   jax.ShapeDtypeStruct((N_TOK, D_MODEL), BF16)],
        compiler_params=_cparams(("parallel",), 48),
        name="out_projection",
    )(merged, w_out, x, gpost, gate, gpre, sc, sh)


def _ffn_kernel(*refs, emit_next, n_p_tiles):
    h_ref, wg_ref, wu_ref, wo_ref, x_ref, gpost_ref, gate_ref = refs[:7]
    if emit_next:
        gpre_ref, sc_ref, sh_ref, x_out_ref, h_out_ref, acc_ref = refs[7:]
    else:
        yp_ref, ys_ref, acc_ref = refs[7:]

    i = pl.program_id(0)
    j = pl.program_id(1)

    @pl.when(j == 0)
    def _():
        acc_ref[...] = jnp.zeros_like(acc_ref)

    h = h_ref[...]
    gate = jnp.dot(h, wg_ref[...], preferred_element_type=F32)
    up = jnp.dot(h, wu_ref[...], preferred_element_type=F32)
    act = (gate * jax.nn.sigmoid(gate) * up).astype(BF16)
    acc_ref[...] += jnp.dot(act, wo_ref[...], preferred_element_type=F32)

    def result():
        return x_ref[...] + gate_ref[...] * (_rms(acc_ref[...]) * gpost_ref[...])

    is_last = j == pl.num_programs(1) - 1
    if emit_next:
        @pl.when(is_last)
        def _():
            x = result()
            x_out_ref[...] = x
            hn = (_rms(x) * gpre_ref[...]) * (1 + sc_ref[...]) + sh_ref[...]
            h_out_ref[...] = hn.astype(BF16)
    else:
        @pl.when(is_last & (i < n_p_tiles))
        def _():
            yp_ref[...] = result()

        @pl.when(is_last & (i >= n_p_tiles))
        def _():
            ys_ref[...] = result()


def _ffn(h, w_in, w_out, layer, x, gpost, gate, nxt=None):
    tm, th = 512, 512
    nj = FF_HIDDEN // th
    npt = N_P // tm
    row = lambda i, j: (_cond_row(i, tm), 0, 0)
    vec = pl.BlockSpec((1, D_MODEL), lambda i, j: (0, 0))
    cond = pl.BlockSpec((None, 1, D_MODEL), row)
    tile = pl.BlockSpec((tm, D_MODEL), lambda i, j: (i, 0))
    in_specs = [
        tile,
        pl.BlockSpec((None, D_MODEL, th), lambda i, j: (layer, 0, j)),
        pl.BlockSpec((None, D_MODEL, th), lambda i, j: (layer, 0, nj + j)),
        pl.BlockSpec((None, th, D_MODEL), lambda i, j: (layer, j, 0)),
        tile, vec, cond,
    ]
    args = [h, w_in, w_in, w_out, x, gpost, gate]
    if nxt is not None:
        in_specs += [vec, cond, cond]
        args += list(nxt)
        out_specs = [tile, tile]
        out_shape = [jax.ShapeDtypeStruct((N_TOK, D_MODEL), F32),
                     jax.ShapeDtypeStruct((N_TOK, D_MODEL), BF16)]
    else:
        out_specs = [
            pl.BlockSpec((tm, D_MODEL), lambda i, j: (jnp.minimum(i, npt - 1), 0)),
            pl.BlockSpec((tm, D_MODEL), lambda i, j: (jnp.maximum(i - npt, 0), 0)),
        ]
        out_shape = [jax.ShapeDtypeStruct((N_P, D_MODEL), F32),
                     jax.ShapeDtypeStruct((N_S, D_MODEL), F32)]
    return pl.pallas_call(
        functools.partial(_ffn_kernel, emit_next=nxt is not None, n_p_tiles=npt),
        grid=(N_TOK // tm, nj),
        in_specs=in_specs,
        out_specs=out_specs,
        out_shape=out_shape,
        scratch_shapes=[pltpu.VMEM((tm, D_MODEL), F32)],
        compiler_params=_cparams(("arbitrary", "arbitrary"), 56),
        name="ffn",
    )(*args)


def kernel(x_prompt, x_sample, cache_k, cache_v, state_C, state_n, state_m, c, c_ctx, w_mod, b_mod, norm_pre1, norm_post1, norm_pre2, norm_post2, w_in, q_norm, k_norm, mlstm_gate_bias, mlstm_norm, conv_w, w_branch, w_out, w_ffn_in, w_ffn_out):
    cond = jnp.concatenate([c_ctx[None], c, jnp.zeros((N_COND - 1 - DEC_BATCH, D_MODEL), F32)], axis=0)
    mod = _modulation(cond, w_mod, b_mod)
    mod = mod.reshape(DEPTH, N_COND, 6, 1, D_MODEL).transpose(0, 2, 1, 3, 4)
    vec = lambda a: a.reshape(1, -1)
    cos, sin = _rope_tables()

    x, h = _prenorm(x_prompt.reshape(N_P, D_MODEL), x_sample.reshape(N_S, D_MODEL),
                    vec(norm_pre1[0]), mod[0, 1], mod[0, 0])

    ck = cache_k.reshape(DEC_BATCH, DEPTH, PAST_LEN, KV_WIDTH)
    cv = cache_v.reshape(DEC_BATCH, DEPTH, PAST_LEN, KV_WIDTH)
    st_n = state_n.reshape(DEC_BATCH, DEPTH, 2, M_HEADS, 1, M_DK)
    w_in_t = jnp.swapaxes(w_in, 1, 2)
    w_branch16 = w_branch.astype(BF16)
    w_out16 = w_out.astype(BF16)
    w_ffn_in16 = w_ffn_in.astype(BF16)
    w_ffn_out16 = w_ffn_out.astype(BF16)

    cache, state = None, None
    for l in range(DEPTH):
        sh1, sc1, g1, sh2, sc2, g2 = (mod[l, i] for i in range(6))
        gate_bias = jnp.pad(mlstm_gate_bias[l], (0, LANES - GATE_COLS)).reshape(1, LANES)
        mnorm_rep = jnp.broadcast_to(mlstm_norm[l][:, None], (M_WIDTH, LANES))

        p32, p16, pt16 = _projection(h, w_in_t, l)
        ld, rows = _gates(h, w_in_t, l, gate_bias)

        qw, kw = vec(q_norm[l]), vec(k_norm[l])
        att_p, new_k, new_v = _attention_ctx(p32, qw, kw, l, cache)
        cache = (new_k, new_v)
        att_s = _attention_dec(p32, ck, cv, qw, kw, cos, sin, l)

        ht_p, *state = _mlstm(p16, pt16, ld, rows, n_seq=BATCH, seq_len=SEQ, row0=0, layer=l,
                              prev_state=state, emit_state=True)
        m0 = jnp.broadcast_to(state_m[:, l].reshape(DEC_BATCH, 2, M_HEADS, 1, 1),
                              (DEC_BATCH, 2, M_HEADS, 1, LANES))
        (ht_s,) = _mlstm(p16, pt16, ld, rows, n_seq=DEC_BATCH, seq_len=DEC_SEQ, row0=N_P, layer=l,
                         init=(state_C, st_n, m0))

        merged = _merge(att_p, att_s, ht_p, ht_s, p16, pt16, w_branch16, l, mnorm_rep,
                        jnp.pad(conv_w[l], ((0, 5), (0, 0))))
        x, h2 = _outproj(merged, w_out16, l, x, vec(norm_post1[l]), g1,
                         vec(norm_pre2[l]), sc2, sh2)
        if l + 1 < DEPTH:
            nxt = (vec(norm_pre1[l + 1]), mod[l + 1, 1], mod[l + 1, 0])
            x, h = _ffn(h2, w_ffn_in16, w_ffn_out16, l, x, vec(norm_post2[l]), g2, nxt)
        else:
            y_p, y_s = _ffn(h2, w_ffn_in16, w_ffn_out16, l, x, vec(norm_post2[l]), g2)

    new_k, new_v = cache
    c_fin, n_fin, m_fin = state
    return (y_p.reshape(BATCH, SEQ, D_MODEL), y_s.reshape(DEC_BATCH, DEC_SEQ, D_MODEL),
            new_k.reshape(BATCH, DEPTH, SEQ, ATT_KV_HEADS, HEAD_DIM),
            new_v.reshape(BATCH, DEPTH, SEQ, ATT_KV_HEADS, HEAD_DIM),
            c_fin, n_fin.reshape(BATCH, DEPTH, 2, M_HEADS, M_DK), m_fin[:, :, :, :, 0, 0])
```

```python
import functools

import jax
import jax.numpy as jnp
from jax import lax
from jax.experimental import pallas as pl
from jax.experimental.pallas import tpu as pltpu

F32 = jnp.float32
BF16 = jnp.bfloat16

D_MODEL = 2048
BATCH = 16
SEQ = 256
DEPTH = 2
DEC_BATCH = 2
DEC_SEQ = 2048
PAST_LEN = 256
GRID_W = 64
EPS = 1e-6
HEAD_DIM = 128
ATT_Q_HEADS = 8
ATT_KV_HEADS = 2
ATT_GROUP = ATT_Q_HEADS // ATT_KV_HEADS
ATT_WIDTH = ATT_Q_HEADS * HEAD_DIM
KV_WIDTH = ATT_KV_HEADS * HEAD_DIM
ROPE_THETA = 10000.0
ROPE_FREQS = HEAD_DIM // 4
M_HEADS = 4
M_DK = 256
M_DV = 256
M_WIDTH = M_HEADS * M_DV
M_CHUNK = 128
CONV_WIDTH = 1024
N_BRANCH = 3
BRANCH_W = 1024
FF_HIDDEN = 5632
IN_WIDTH = 14864

LANES = 128
N_P = BATCH * SEQ
N_S = DEC_BATCH * DEC_SEQ
N_TOK = N_P + N_S
N_CHUNKS = N_TOK // M_CHUNK
N_COND = 8
KV_LEN_S = PAST_LEN + DEC_SEQ

W_AQ, W_AK, W_AV = 0, 1024, 1280
W_MQ, W_MV, W_MG, W_CB = 1536, 3584, 5632, 5648
GATE_COLS = 4 * M_HEADS
PROJ_TN = 512
N_TILE_A = W_MG // PROJ_TN
N_TILE_C = (IN_WIDTH - W_CB) // PROJ_TN
N_TILE_32 = W_MQ // PROJ_TN
TILE_T0 = W_MV // PROJ_TN
TILE_T1 = N_TILE_A

P32_W = W_MQ
C16_GL, C16_MQ, C16_MK = 0, 6144, 7168
C16_CB, C16_CC, C16_CX = 8192, 9216, 10240
P16_W = 11264
RT_MV, RT_MO = 0, 1024
PT_H = 2048
M_AUG = M_DV + LANES
GATE_ROWS = 24

NT_DIMS = (((1,), (1,)), ((), ()))
TN_DIMS = (((0,), (0,)), ((), ()))


def _cparams(semantics, vmem_mb):
    return pltpu.CompilerParams(dimension_semantics=semantics,
                                vmem_limit_bytes=vmem_mb * 1024 * 1024)


def _rms(x):
    return x * lax.rsqrt(jnp.mean(x * x, axis=-1, keepdims=True) + EPS)


def _cond_row(tile, tm):
    return jnp.where(tile < N_P // tm, 0, 1 + (tile * tm - N_P) // DEC_SEQ)


def _mod_kernel(c_ref, w_ref, b_ref, o_ref):
    c = c_ref[...]
    a = (c * jax.nn.sigmoid(c)).astype(BF16)
    o_ref[...] = jnp.dot(a, w_ref[...].astype(BF16), preferred_element_type=F32) + b_ref[...]


def _modulation(cond, w_mod, b_mod):
    tn = 1024
    n = 6 * D_MODEL
    return pl.pallas_call(
        _mod_kernel,
        grid=(DEPTH, n // tn),
        in_specs=[
            pl.BlockSpec((N_COND, D_MODEL), lambda l, j: (0, 0)),
            pl.BlockSpec((None, D_MODEL, tn), lambda l, j: (l, 0, j)),
            pl.BlockSpec((None, 1, tn), lambda l, j: (l, 0, j)),
        ],
        out_specs=pl.BlockSpec((None, N_COND, tn), lambda l, j: (l, 0, j)),
        out_shape=jax.ShapeDtypeStruct((DEPTH, N_COND, n), F32),
        compiler_params=_cparams(("parallel", "parallel"), 32),
        name="modulation",
    )(cond, w_mod, b_mod.reshape(DEPTH, 1, n))


def _prenorm_kernel(xp_ref, xs_ref, g_ref, sc_ref, sh_ref, x_ref, h_ref, *, n_p_tiles):
    def emit(src_ref):
        x = src_ref[...]
        x_ref[...] = x
        y = _rms(x) * g_ref[...]
        h_ref[...] = (y * (1 + sc_ref[...]) + sh_ref[...]).astype(BF16)

    i = pl.program_id(0)
    pl.when(i < n_p_tiles)(lambda: emit(xp_ref))
    pl.when(i >= n_p_tiles)(lambda: emit(xs_ref))


def _prenorm(xp, xs, g, sc, sh):
    tm = 512
    npt = N_P // tm
    row = lambda i: (_cond_row(i, tm), 0, 0)
    tile = pl.BlockSpec((tm, D_MODEL), lambda i: (i, 0))
    return pl.pallas_call(
        functools.partial(_prenorm_kernel, n_p_tiles=npt),
        grid=(N_TOK // tm,),
        in_specs=[
            pl.BlockSpec((tm, D_MODEL), lambda i: (jnp.minimum(i, npt - 1), 0)),
            pl.BlockSpec((tm, D_MODEL), lambda i: (jnp.maximum(i - npt, 0), 0)),
            pl.BlockSpec((1, D_MODEL), lambda i: (0, 0)),
            pl.BlockSpec((None, 1, D_MODEL), row),
            pl.BlockSpec((None, 1, D_MODEL), row),
        ],
        out_specs=[tile, tile],
        out_shape=[jax.ShapeDtypeStruct((N_TOK, D_MODEL), F32),
                   jax.ShapeDtypeStruct((N_TOK, D_MODEL), BF16)],
        compiler_params=_cparams(("arbitrary",), 32),
        name="prenorm",
    )(xp, xs, g, sc, sh)


def _proj_tile16(j):
    first_gl = N_TILE_A + 3 * CONV_WIDTH // PROJ_TN
    mq0 = C16_MQ // PROJ_TN
    return jnp.where(j < TILE_T0, mq0 + jnp.maximum(j - N_TILE_32, 0),
                     jnp.where(j < TILE_T1, mq0 + TILE_T0 - N_TILE_32 - 1,
                               jnp.where(j < first_gl, C16_CB // PROJ_TN + (j - N_TILE_A),
                                         j - first_gl)))


def _proj_kernel(x_ref, w_ref, tail_ref, o32_ref, o16_ref, ot_ref, w_scr):
    j = pl.program_id(1)
    shifted = j >= N_TILE_A
    off = pl.multiple_of(jnp.where(shifted, GATE_COLS, 0), 8)
    body = PROJ_TN - GATE_COLS
    w_scr[0:body, :] = w_ref[pl.ds(off, body), :].astype(BF16)
    w_scr[body:, :] = jnp.where(shifted, tail_ref[...], w_ref[body:, :]).astype(BF16)

    transposed = (j >= TILE_T0) & (j < TILE_T1)

    @pl.when(transposed)
    def _():
        acc_t = lax.dot_general(w_scr[...], x_ref[...], NT_DIMS, preferred_element_type=F32)
        ot_ref[...] = acc_t.astype(BF16)

    @pl.when(j < N_TILE_32)
    def _():
        o32_ref[...] = lax.dot_general(x_ref[...], w_scr[...], NT_DIMS, preferred_element_type=F32)

    @pl.when((j >= N_TILE_32) & jnp.logical_not(transposed))
    def _():
        acc = lax.dot_general(x_ref[...], w_scr[...], NT_DIMS, preferred_element_type=F32)
        o16_ref[...] = acc.astype(BF16)


def _projection(h, w_t, layer):
    tm = 2048
    nj = N_TILE_A + N_TILE_C
    tails_per_tile = PROJ_TN // GATE_COLS
    tail0 = W_MG // GATE_COLS

    def tail_idx(i, j):
        return (layer, jnp.where(j < N_TILE_A, tail0, (j + 1) * tails_per_tile), 0)

    return pl.pallas_call(
        _proj_kernel,
        grid=(N_TOK // tm, nj),
        in_specs=[
            pl.BlockSpec((tm, D_MODEL), lambda i, j: (i, 0), pipeline_mode=pl.Buffered(1)),
            pl.BlockSpec((None, PROJ_TN, D_MODEL), lambda i, j: (layer, j, 0)),
            pl.BlockSpec((None, GATE_COLS, D_MODEL), tail_idx),
        ],
        out_specs=[
            pl.BlockSpec((tm, PROJ_TN), lambda i, j: (i, jnp.minimum(j, N_TILE_32 - 1))),
            pl.BlockSpec((tm, PROJ_TN), lambda i, j: (i, _proj_tile16(j))),
            pl.BlockSpec((PROJ_TN, tm),
                         lambda i, j: (jnp.clip(j - TILE_T0, 0, TILE_T1 - TILE_T0 - 1), i)),
        ],
        out_shape=[jax.ShapeDtypeStruct((N_TOK, P32_W), F32),
                   jax.ShapeDtypeStruct((N_TOK, P16_W), BF16),
                   jax.ShapeDtypeStruct((PT_H, N_TOK), BF16)],
        scratch_shapes=[pltpu.VMEM((PROJ_TN, D_MODEL), BF16)],
        compiler_params=_cparams(("arbitrary", "arbitrary"), 52),
        name="projection",
    )(h, w_t, w_t)


def _gate_kernel(h_ref, w_ref, b_ref, ld_ref, row_ref, *, chunks):
    g = lax.dot_general(h_ref[...], w_ref[...].astype(BF16), NT_DIMS,
                        preferred_element_type=F32) + b_ref[...]
    lf = jax.nn.log_sigmoid(g)
    s_idx = lax.broadcasted_iota(jnp.int32, (M_CHUNK, M_CHUNK), 0)
    t_idx = lax.broadcasted_iota(jnp.int32, (M_CHUNK, M_CHUNK), 1)
    tril = (t_idx <= s_idx).astype(F32)
    triu = (t_idx >= s_idx).astype(F32)
    ones = jnp.ones((M_CHUNK, M_CHUNK), F32)
    lane = lax.broadcasted_iota(jnp.int32, (M_CHUNK, LANES), 1)
    exact = dict(preferred_element_type=F32, precision=lax.Precision.HIGHEST)
    for ch in range(chunks):
        rows = slice(ch * M_CHUNK, (ch + 1) * M_CHUNK)
        gc = g[rows]
        lfc = lf[rows]
        pre = jnp.dot(tril, lfc, **exact)
        suf = jnp.dot(triu, lfc, **exact)
        tot = jnp.dot(ones, lfc, **exact)
        both = jnp.where((lane >= 4) & (lane < 8), pre,
                         jnp.where((lane >= 12) & (lane < 16), suf, gc))
        both_t = both.T
        tot_t = tot.T
        for d in range(2):
            mask = (s_idx <= t_idx) if d == 0 else (s_idx >= t_idx)
            row_ref[d, ch, 20:GATE_ROWS, :] = jnp.zeros((GATE_ROWS - 20, M_CHUNK), F32)
            for hd in range(M_HEADS):
                li, lb = 8 * d + hd, 8 * d + 4 + hd
                i_row = both_t[li:li + 1, :]
                b_row = both_t[lb:lb + 1, :]
                bl_row = tot_t[lb:lb + 1, :]
                key = both[:, lb:lb + 1] - both[:, li:li + 1]
                ld = jnp.where(mask, b_row - key, -jnp.inf)
                wl = bl_row - b_row + i_row
                ld_ref[d, hd, rows, :] = ld
                row_ref[d, ch, hd:hd + 1, :] = i_row
                row_ref[d, ch, 4 + hd:5 + hd, :] = b_row
                row_ref[d, ch, 8 + hd:9 + hd, :] = jnp.max(ld, axis=0, keepdims=True)
                row_ref[d, ch, 12 + hd:13 + hd, :] = bl_row
                row_ref[d, ch, 16 + hd:17 + hd, :] = jnp.broadcast_to(
                    jnp.max(wl, axis=-1, keepdims=True), (1, M_CHUNK))


def _gates(h, w_t, layer, bias):
    tm = 512
    chunks = tm // M_CHUNK
    return pl.pallas_call(
        functools.partial(_gate_kernel, chunks=chunks),
        grid=(N_TOK // tm,),
        in_specs=[
            pl.BlockSpec((tm, D_MODEL), lambda i: (i, 0)),
            pl.BlockSpec((None, LANES, D_MODEL), lambda i: (layer, W_MG // LANES, 0)),
            pl.BlockSpec((1, LANES), lambda i: (0, 0)),
        ],
        out_specs=[
            pl.BlockSpec((2, M_HEADS, tm, M_CHUNK), lambda i: (0, 0, i, 0)),
            pl.BlockSpec((2, chunks, GATE_ROWS, M_CHUNK), lambda i: (0, i, 0, 0)),
        ],
        out_shape=[
            jax.ShapeDtypeStruct((2, M_HEADS, N_TOK, M_CHUNK), F32),
            jax.ShapeDtypeStruct((2, N_CHUNKS, GATE_ROWS, M_CHUNK), F32),
        ],
        compiler_params=_cparams(("parallel",), 32),
        name="mlstm_gates",
    )(h, w_t, bias)


def _mlstm_kernel(*refs, n_par, has_init, emit_state, has_prev):
    pos = 0
    seq_refs = []
    for _ in range(n_par):
        seq_refs.append(refs[pos:pos + 5])
        pos += 5
    if has_init:
        c0_ref, n0_ref, m0_ref = refs[pos:pos + 3]
        pos += 3
    if has_prev:
        pos += 3
    h_ref = refs[pos]
    pos += 1
    if emit_state:
        cout_ref, nout_ref, mout_ref = refs[pos:pos + 3]
        pos += 3
    c_scr, m_scr = refs[pos:pos + 2]

    ci = pl.program_id(2)
    last = pl.num_programs(2) - 1
    pad_rows = lax.broadcasted_iota(jnp.int32, (LANES, M_CHUNK), 0)
    one_row = (pad_rows == 0).astype(F32)

    @pl.when(ci == 0)
    def _():
        if has_init:
            for u in range(n_par):
                for hd in range(M_HEADS):
                    c_scr[u, hd, 0:M_DV, :] = c0_ref[u, hd].T
                    pad = lax.broadcasted_iota(jnp.int32, (LANES, M_DK), 0)
                    c_scr[u, hd, M_DV:M_AUG, :] = jnp.where(pad == 0, n0_ref[u, hd], 0.0)
            m_scr[...] = m0_ref[...]
        else:
            c_scr[...] = jnp.zeros_like(c_scr)
            m_scr[...] = jnp.zeros_like(m_scr)

    for u in range(n_par):
        q_ref, k_ref, vt_ref, ld_ref, row_ref = seq_refs[u]
        for hd in range(M_HEADS):
            cols = slice(hd * M_DK, (hd + 1) * M_DK)
            q = q_ref[:, cols]
            k = k_ref[:, cols] * (M_DK ** -0.5)
            vt_aug = jnp.concatenate([vt_ref[cols, :].astype(F32), one_row], axis=0)
            ld = ld_ref[hd]
            i_row = row_ref[hd:hd + 1, :]
            b_row = row_ref[4 + hd:5 + hd, :]
            ldmax_row = row_ref[8 + hd:9 + hd, :]
            bl_row = row_ref[12 + hd:13 + hd, :]
            wmax_row = row_ref[16 + hd:17 + hd, :]
            mem = c_scr[u, hd]
            m_prev = m_scr[u, hd]

            g_row = b_row + m_prev
            mt_row = jnp.maximum(g_row, ldmax_row)
            kq = lax.dot_general(k, q, NT_DIMS, preferred_element_type=F32)
            st = (kq * jnp.exp(ld - mt_row)).astype(BF16)
            nd = jnp.exp(g_row - mt_row) * lax.dot_general(
                mem.astype(BF16), q, NT_DIMS, preferred_element_type=F32) \
                + jnp.dot(vt_aug.astype(BF16), st, preferred_element_type=F32)
            den_row = nd[M_DV:M_DV + 1, :]
            scale_row = 1.0 / jnp.maximum(jnp.abs(den_row), jnp.exp(-mt_row))
            h_ref[u, cols, :] = nd[0:M_DV, :] * scale_row

            m_new = jnp.maximum(bl_row + m_prev, wmax_row)
            w_row = jnp.exp(bl_row - b_row + i_row - m_new)
            dec = jnp.exp(bl_row + m_prev - m_new)
            dec_wide = jnp.concatenate([dec] * (M_DK // LANES), axis=1)
            c_scr[u, hd] = dec_wide * mem + jnp.dot((vt_aug * w_row).astype(BF16), k,
                                                    preferred_element_type=F32)
            m_scr[u, hd] = m_new

    if emit_state:
        @pl.when(ci == last)
        def _():
            if has_prev:
                slot = lambda ref: ref
            else:
                slot = lambda ref: ref.at[:, 0]
                for ref in (cout_ref, nout_ref, mout_ref):
                    for l in range(1, DEPTH):
                        ref[:, l] = jnp.zeros(ref.shape[:1] + ref.shape[2:], F32)
            for u in range(n_par):
                for hd in range(M_HEADS):
                    slot(cout_ref)[u, hd] = c_scr[u, hd, 0:M_DV, :].T
                    slot(nout_ref)[u, hd] = c_scr[u, hd, M_DV:M_DV + 1, :]
            slot(mout_ref)[...] = m_scr[...]


def _mlstm(p16, pt16, ld, rows, *, n_seq, seq_len, row0, layer, init=None, prev_state=None,
           emit_state=False):
    n_par = 2
    nc = seq_len // M_CHUNK
    blk0 = row0 // M_CHUNK

    def chunk_of(c, d):
        return jnp.where(d == 0, c, nc - 1 - c)

    in_specs, args = [], []
    for u in range(n_par):
        blk = lambda g, d, c, u=u: blk0 + (g * n_par + u) * nc + chunk_of(c, d)
        in_specs += [
            pl.BlockSpec((M_CHUNK, M_WIDTH), lambda g, d, c, blk=blk: (blk(g, d, c), C16_MQ // M_WIDTH)),
            pl.BlockSpec((M_CHUNK, M_WIDTH), lambda g, d, c, blk=blk: (blk(g, d, c), C16_MK // M_WIDTH)),
            pl.BlockSpec((M_WIDTH, M_CHUNK), lambda g, d, c, blk=blk: (RT_MV // M_WIDTH, blk(g, d, c))),
            pl.BlockSpec((None, M_HEADS, M_CHUNK, M_CHUNK), lambda g, d, c, blk=blk: (d, 0, blk(g, d, c), 0)),
            pl.BlockSpec((None, None, GATE_ROWS, M_CHUNK), lambda g, d, c, blk=blk: (d, blk(g, d, c), 0, 0)),
        ]
        args += [p16, p16, pt16, ld, rows]
    if init is not None:
        in_specs += [
            pl.BlockSpec((n_par, None, None, M_HEADS, M_DK, M_DV), lambda g, d, c: (g, layer, d, 0, 0, 0)),
            pl.BlockSpec((n_par, None, None, M_HEADS, 1, M_DK), lambda g, d, c: (g, layer, d, 0, 0, 0)),
            pl.BlockSpec((n_par, None, M_HEADS, 1, LANES), lambda g, d, c: (g, d, 0, 0, 0)),
        ]
        args += list(init)
    aliases = {}
    if prev_state is not None:
        for k_, a in enumerate(prev_state):
            aliases[len(args)] = 1 + k_
            in_specs.append(pl.BlockSpec(memory_space=pl.ANY))
            args.append(a)
    out_specs = [pl.BlockSpec((None, n_par, None, M_WIDTH, M_CHUNK),
                              lambda g, d, c: (d, g, chunk_of(c, d), 0, 0))]
    out_shape = [jax.ShapeDtypeStruct((2, n_seq, nc, M_WIDTH, M_CHUNK), F32)]
    if emit_state:
        if prev_state is None:
            lblk, lidx = DEPTH, 0
        else:
            lblk, lidx = None, layer
        out_specs += [
            pl.BlockSpec((n_par, lblk, None, M_HEADS, M_DK, M_DV), lambda g, d, c: (g, lidx, d, 0, 0, 0)),
            pl.BlockSpec((n_par, lblk, None, M_HEADS, 1, M_DK), lambda g, d, c: (g, lidx, d, 0, 0, 0)),
            pl.BlockSpec((n_par, lblk, None, M_HEADS, 1, LANES), lambda g, d, c: (g, lidx, d, 0, 0, 0)),
        ]
        out_shape += [
            jax.ShapeDtypeStruct((n_seq, DEPTH, 2, M_HEADS, M_DK, M_DV), F32),
            jax.ShapeDtypeStruct((n_seq, DEPTH, 2, M_HEADS, 1, M_DK), F32),
            jax.ShapeDtypeStruct((n_seq, DEPTH, 2, M_HEADS, 1, LANES), F32),
        ]
    return pl.pallas_call(
        functools.partial(_mlstm_kernel, n_par=n_par, has_init=init is not None,
                          emit_state=emit_state, has_prev=prev_state is not None),
        grid=(n_seq // n_par, 2, nc),
        in_specs=in_specs,
        out_specs=out_specs,
        out_shape=out_shape,
        input_output_aliases=aliases,
        scratch_shapes=[
            pltpu.VMEM((n_par, M_HEADS, M_AUG, M_DK), F32),
            pltpu.VMEM((n_par, M_HEADS, 1, LANES), F32),
        ],
        compiler_params=_cparams(("arbitrary", "arbitrary", "arbitrary"), 40),
        name="mlstm_scan_dec" if init is not None else "mlstm_scan_ctx",
    )(*args)


def _swap32(x):
    lane = lax.broadcasted_iota(jnp.int32, x.shape, 1)
    return jnp.where((lane % 64) < 32, pltpu.roll(x, HEAD_DIM - 32, axis=1),
                     pltpu.roll(x, 32, axis=1))


def _group_attention(q_heads, kb, vb, o_ref):
    for g, q in enumerate(q_heads):
        s = lax.dot_general(q.astype(BF16), kb, NT_DIMS, preferred_element_type=F32)
        e = jnp.exp2(s - jnp.max(s, axis=-1, keepdims=True))
        l = jnp.sum(e, axis=-1, keepdims=True)
        o = jnp.dot(e.astype(BF16), vb, preferred_element_type=F32) / l
        o_ref[:, g * HEAD_DIM:(g + 1) * HEAD_DIM] = o.astype(BF16)


Q_SCALE = HEAD_DIM ** -0.5 * 1.4426950408889634


def _attn_ctx_kernel(*refs, first_layer):
    q_ref, k_ref, v_ref, qw_ref, kw_ref = refs[:5]
    o_ref, kout_ref, vout_ref = refs[-3:]
    k = _rms(k_ref[...]) * kw_ref[...]
    v = v_ref[...]
    if first_layer:
        kout_ref[0] = k
        vout_ref[0] = v
        for l in range(1, DEPTH):
            kout_ref[l] = jnp.zeros_like(k)
            vout_ref[l] = jnp.zeros_like(v)
    else:
        kout_ref[...] = k
        vout_ref[...] = v
    q_heads = [_rms(q_ref[:, g * HEAD_DIM:(g + 1) * HEAD_DIM]) * qw_ref[...] * Q_SCALE
               for g in range(ATT_GROUP)]
    _group_attention(q_heads, k.astype(BF16), v.astype(BF16), o_ref)


def _attention_ctx(p32, qw, kw, layer, prev_cache=None):
    gw = ATT_GROUP * HEAD_DIM
    in_specs = [
        pl.BlockSpec((SEQ, gw), lambda b, h: (b, W_AQ // gw + h)),
        pl.BlockSpec((SEQ, HEAD_DIM), lambda b, h: (b, W_AK // HEAD_DIM + h)),
        pl.BlockSpec((SEQ, HEAD_DIM), lambda b, h: (b, W_AV // HEAD_DIM + h)),
        pl.BlockSpec((1, HEAD_DIM), lambda b, h: (0, 0)),
        pl.BlockSpec((1, HEAD_DIM), lambda b, h: (0, 0)),
    ]
    args = [p32, p32, p32, qw, kw]
    aliases = {}
    if prev_cache is not None:
        for k_, a in enumerate(prev_cache):
            aliases[len(args)] = 1 + k_
            in_specs.append(pl.BlockSpec(memory_space=pl.ANY))
            args.append(a)
    if prev_cache is None:
        cache_spec = pl.BlockSpec((None, DEPTH, SEQ, HEAD_DIM), lambda b, h: (b, 0, 0, h))
    else:
        cache_spec = pl.BlockSpec((None, None, SEQ, HEAD_DIM), lambda b, h: (b, layer, 0, h))
    cache_shape = jax.ShapeDtypeStruct((BATCH, DEPTH, SEQ, KV_WIDTH), F32)
    return pl.pallas_call(
        functools.partial(_attn_ctx_kernel, first_layer=prev_cache is None),
        grid=(BATCH, ATT_KV_HEADS),
        in_specs=in_specs,
        out_specs=[pl.BlockSpec((SEQ, gw), lambda b, h: (b, h)), cache_spec, cache_spec],
        out_shape=[jax.ShapeDtypeStruct((N_P, ATT_WIDTH), BF16), cache_shape, cache_shape],
        input_output_aliases=aliases,
        compiler_params=_cparams(("arbitrary", "arbitrary"), 32),
        name="attention_ctx",
    )(*args)


def _attn_dec_kernel(q_ref, k_ref, v_ref, ck_ref, cv_ref, qw_ref, kw_ref,
                     cos_ref, sin_ref, cosq_ref, sinq_ref, o_ref, k_scr, v_scr):
    @pl.when(pl.program_id(2) == 0)
    def _():
        k_scr[0:PAST_LEN, :] = ck_ref[...].astype(BF16)
        v_scr[0:PAST_LEN, :] = cv_ref[...].astype(BF16)
        k = _rms(k_ref[...]) * kw_ref[...]
        k = k * cos_ref[...] + _swap32(k) * sin_ref[...]
        k_scr[PAST_LEN:, :] = k.astype(BF16)
        v_scr[PAST_LEN:, :] = v_ref[...].astype(BF16)

    q_heads = []
    for g in range(ATT_GROUP):
        q = _rms(q_ref[:, g * HEAD_DIM:(g + 1) * HEAD_DIM]) * qw_ref[...]
        q_heads.append((q * cosq_ref[...] + _swap32(q) * sinq_ref[...]) * Q_SCALE)
    _group_attention(q_heads, k_scr[...], v_scr[...], o_ref)


def _attention_dec(p32, ck, cv, qw, kw, cos, sin, layer):
    tq = 128
    gw = ATT_GROUP * HEAD_DIM
    nq = DEC_SEQ // tq
    seq0 = N_P // DEC_SEQ
    cache_spec = pl.BlockSpec((None, None, PAST_LEN, HEAD_DIM), lambda b, h, i: (b, layer, 0, h))
    return pl.pallas_call(
        _attn_dec_kernel,
        grid=(DEC_BATCH, ATT_KV_HEADS, nq),
        in_specs=[
            pl.BlockSpec((tq, gw), lambda b, h, i: (N_P // tq + b * nq + i, W_AQ // gw + h)),
            pl.BlockSpec((DEC_SEQ, HEAD_DIM), lambda b, h, i: (seq0 + b, W_AK // HEAD_DIM + h)),
            pl.BlockSpec((DEC_SEQ, HEAD_DIM), lambda b, h, i: (seq0 + b, W_AV // HEAD_DIM + h)),
            cache_spec, cache_spec,
            pl.BlockSpec((1, HEAD_DIM), lambda b, h, i: (0, 0)),
            pl.BlockSpec((1, HEAD_DIM), lambda b, h, i: (0, 0)),
            pl.BlockSpec((DEC_SEQ, HEAD_DIM), lambda b, h, i: (0, 0)),
            pl.BlockSpec((DEC_SEQ, HEAD_DIM), lambda b, h, i: (0, 0)),
            pl.BlockSpec((tq, HEAD_DIM), lambda b, h, i: (i, 0)),
            pl.BlockSpec((tq, HEAD_DIM), lambda b, h, i: (i, 0)),
        ],
        out_specs=pl.BlockSpec((tq, gw), lambda b, h, i: (b * nq + i, h)),
        out_shape=jax.ShapeDtypeStruct((N_S, ATT_WIDTH), BF16),
        scratch_shapes=[
            pltpu.VMEM((KV_LEN_S, HEAD_DIM), BF16),
            pltpu.VMEM((KV_LEN_S, HEAD_DIM), BF16),
        ],
        compiler_params=_cparams(("arbitrary", "arbitrary", "arbitrary"), 32),
        name="attention_dec",
    )(p32, p32, p32, ck, cv, qw, kw, cos, sin, cos, sin)


def _rope_tables():
    t = jnp.arange(DEC_SEQ)
    inv = ROPE_THETA ** (-jnp.arange(ROPE_FREQS, dtype=F32) / ROPE_FREQS)
    ang = jnp.stack([t // GRID_W, t % GRID_W], axis=-1).astype(F32)[:, :, None] * inv
    cos = jnp.cos(ang)
    sin = jnp.sin(ang)
    cos = jnp.stack([cos, cos], axis=2).reshape(DEC_SEQ, HEAD_DIM)
    sin = jnp.stack([-sin, sin], axis=2).reshape(DEC_SEQ, HEAD_DIM)
    return cos, sin


def _merge_kernel(attp_ref, atts_ref, hfp_ref, hbp_ref, hfs_ref, hbs_ref,
                  mot_ref, cb_ref, cc_ref, cx_ref, ccp_ref, cxp_ref, ccn_ref, cxn_ref,
                  gl0_ref, gl1_ref, gl2_ref, wb_ref, mn_ref, cw_ref, o_ref, *, tm, halo):
    i = pl.program_id(0)
    row0 = i * tm
    in_p = row0 < N_P
    off = jnp.where(in_p, row0 % SEQ, (row0 - N_P) % DEC_SEQ)
    seq_len = jnp.where(in_p, SEQ, DEC_SEQ)
    first = off == 0
    final = off + tm == seq_len

    hsum = jnp.where(in_p, hfp_ref[...] + hbp_ref[...], hfs_ref[...] + hbs_ref[...])
    mn = mn_ref[...]
    chunk_parts = []
    for ch in range(tm // M_CHUNK):
        parts = []
        for hd in range(M_HEADS):
            x = hsum[ch, hd * M_DV:(hd + 1) * M_DV, :]
            parts.append(x * lax.rsqrt(jnp.mean(x * x, axis=0, keepdims=True) + EPS))
        chunk_parts.append(jnp.concatenate(parts, axis=0) * mn)
    ml_t = jnp.concatenate(chunk_parts, axis=1) * jax.nn.sigmoid(mot_ref[...].astype(F32))

    u = cc_ref[...].astype(F32) * cx_ref[...].astype(F32)
    u_halo_prev = ccp_ref[...].astype(F32) * cxp_ref[...].astype(F32)
    u_halo_next = ccn_ref[...].astype(F32) * cxn_ref[...].astype(F32)
    u_prev_row = jnp.where(first, 0.0, u_halo_prev[halo - 1:halo, :])
    u_next_row = jnp.where(final, 0.0, u_halo_next[0:1, :])
    ridx = lax.broadcasted_iota(jnp.int32, u.shape, 0)
    u_prev = jnp.where(ridx == 0, u_prev_row, pltpu.roll(u, 1, axis=0))
    u_next = jnp.where(ridx == tm - 1, u_next_row, pltpu.roll(u, tm - 1, axis=0))
    cw = cw_ref[...]
    cv = cb_ref[...].astype(F32) * (cw[0:1, :] * u_prev + cw[1:2, :] * u + cw[2:3, :] * u_next)

    att = jnp.where(in_p, attp_ref[...], atts_ref[...])
    acc = jax.nn.sigmoid(gl0_ref[...].astype(F32)) * jnp.dot(att, wb_ref[0],
                                                             preferred_element_type=F32)
    acc += jax.nn.sigmoid(gl1_ref[...].astype(F32)) * lax.dot_general(
        ml_t.astype(BF16), wb_ref[1], TN_DIMS, preferred_element_type=F32)
    acc += jax.nn.sigmoid(gl2_ref[...].astype(F32)) * jnp.dot(cv.astype(BF16), wb_ref[2],
                                                              preferred_element_type=F32)
    o_ref[...] = acc.astype(BF16)


def _merge(att_p, att_s, ht_p, ht_s, p16, pt16, wb, layer, mnorm_rep, convw):
    tm = 2 * M_CHUNK
    cpt = tm // M_CHUNK
    halo = 16
    nt = N_TOK // tm
    npt = N_P // tm
    th = tm // halo
    last_h = N_TOK // halo - 1
    p_idx = lambda i: jnp.minimum(i, npt - 1)
    s_idx = lambda i: jnp.maximum(i - npt, 0)
    cblk = lambda col: (lambda i: (i, col // BRANCH_W))
    prevh = lambda col: (lambda i: (jnp.maximum(i * th - 1, 0), col // BRANCH_W))
    nexth = lambda col: (lambda i: (jnp.minimum((i + 1) * th, last_h), col // BRANCH_W))
    glblk = lambda g: (lambda i: (i, C16_GL // D_MODEL + g))
    ht_p = ht_p.reshape(2, N_P // M_CHUNK, M_WIDTH, M_CHUNK)
    ht_s = ht_s.reshape(2, N_S // M_CHUNK, M_WIDTH, M_CHUNK)
    ht_spec = lambda d, idx: pl.BlockSpec((None, cpt, M_WIDTH, M_CHUNK), lambda i: (d, idx(i), 0, 0))
    return pl.pallas_call(
        functools.partial(_merge_kernel, tm=tm, halo=halo),
        grid=(nt,),
        in_specs=[
            pl.BlockSpec((tm, BRANCH_W), lambda i: (p_idx(i), 0)),
            pl.BlockSpec((tm, BRANCH_W), lambda i: (s_idx(i), 0)),
            ht_spec(0, p_idx), ht_spec(1, p_idx), ht_spec(0, s_idx), ht_spec(1, s_idx),
            pl.BlockSpec((M_WIDTH, tm), lambda i: (RT_MO // M_WIDTH, i)),
            pl.BlockSpec((tm, BRANCH_W), cblk(C16_CB)),
            pl.BlockSpec((tm, BRANCH_W), cblk(C16_CC)),
            pl.BlockSpec((tm, BRANCH_W), cblk(C16_CX)),
            pl.BlockSpec((halo, BRANCH_W), prevh(C16_CC)),
            pl.BlockSpec((halo, BRANCH_W), prevh(C16_CX)),
            pl.BlockSpec((halo, BRANCH_W), nexth(C16_CC)),
            pl.BlockSpec((halo, BRANCH_W), nexth(C16_CX)),
            pl.BlockSpec((tm, D_MODEL), glblk(0)),
            pl.BlockSpec((tm, D_MODEL), glblk(1)),
            pl.BlockSpec((tm, D_MODEL), glblk(2)),
            pl.BlockSpec((None, N_BRANCH, BRANCH_W, D_MODEL), lambda i: (layer, 0, 0, 0),
                         pipeline_mode=pl.Buffered(1)),
            pl.BlockSpec((M_WIDTH, LANES), lambda i: (0, 0)),
            pl.BlockSpec((8, CONV_WIDTH), lambda i: (0, 0)),
        ],
        out_specs=pl.BlockSpec((tm, D_MODEL), lambda i: (i, 0)),
        out_shape=jax.ShapeDtypeStruct((N_TOK, D_MODEL), BF16),
        compiler_params=_cparams(("arbitrary",), 52),
        name="branch_merge",
    )(att_p, att_s, ht_p, ht_p, ht_s, ht_s, pt16, p16, p16, p16, p16, p16, p16, p16,
      p16, p16, p16, wb, mnorm_rep, convw)


def _outproj_kernel(m_ref, w_ref, x_ref, gpost_ref, gate_ref, gpre_ref, sc_ref, sh_ref,
                    x_out_ref, h_out_ref):
    mix = jnp.dot(m_ref[...], w_ref[...], preferred_element_type=F32)
    x = x_ref[...] + gate_ref[...] * (_rms(mix) * gpost_ref[...])
    x_out_ref[...] = x
    h = (_rms(x) * gpre_ref[...]) * (1 + sc_ref[...]) + sh_ref[...]
    h_out_ref[...] = h.astype(BF16)


def _outproj(merged, w_out, layer, x, gpost, gate, gpre, sc, sh):
    tm = 512
    row = lambda i: (_cond_row(i, tm), 0, 0)
    vec = pl.BlockSpec((1, D_MODEL), lambda i: (0, 0))
    cond = pl.BlockSpec((None, 1, D_MODEL), row)
    tile = pl.BlockSpec((tm, D_MODEL), lambda i: (i, 0))
    return pl.pallas_call(
        _outproj_kernel,
        grid=(N_TOK // tm,),
        in_specs=[tile,
                  pl.BlockSpec((None, D_MODEL, D_MODEL), lambda i: (layer, 0, 0),
                               pipeline_mode=pl.Buffered(1)),
                  tile, vec, cond, vec, cond, cond],
        out_specs=[tile, tile],
        out_shape=[jax.ShapeDtypeStruct((N_TOK, D_MODEL), F32),
                   jax.ShapeDtypeStruct((N_TOK, D_MODEL), BF16)],
        compiler_params=_cparams(("parallel",), 48),
        name="out_projection",
    )(merged, w_out, x, gpost, gate, gpre, sc, sh)


def _ffn_kernel(*refs, emit_next, n_p_tiles):
    h_ref, wg_ref, wu_ref, wo_ref, x_ref, gpost_ref, gate_ref = refs[:7]
    if emit_next:
        gpre_ref, sc_ref, sh_ref, x_out_ref, h_out_ref = refs[7:]
        acc_ref = x_out_ref
    else:
        yp_ref, ys_ref, acc_ref = refs[7:]

    i = pl.program_id(0)
    j = pl.program_id(1)

    @pl.when(j == 0)
    def _():
        acc_ref[...] = jnp.zeros_like(acc_ref)

    h = h_ref[...]
    gate = jnp.dot(h, wg_ref[...], preferred_element_type=F32)
    up = jnp.dot(h, wu_ref[...], preferred_element_type=F32)
    act = (gate * jax.nn.sigmoid(gate) * up).astype(BF16)
    acc_ref[...] += jnp.dot(act, wo_ref[...], preferred_element_type=F32)

    def result():
        return x_ref[...] + gate_ref[...] * (_rms(acc_ref[...]) * gpost_ref[...])

    is_last = j == pl.num_programs(1) - 1
    if emit_next:
        @pl.when(is_last)
        def _():
            x = result()
            x_out_ref[...] = x
            hn = (_rms(x) * gpre_ref[...]) * (1 + sc_ref[...]) + sh_ref[...]
            h_out_ref[...] = hn.astype(BF16)
    else:
        @pl.when(is_last & (i < n_p_tiles))
        def _():
            yp_ref[...] = result()

        @pl.when(is_last & (i >= n_p_tiles))
        def _():
            ys_ref[...] = result()


def _ffn(h, w_in, w_out, layer, x, gpost, gate, nxt=None):
    tm = 1024 if nxt is not None else 512
    th = 512
    nj = FF_HIDDEN // th
    npt = N_P // tm
    row = lambda i, j: (_cond_row(i, tm), 0, 0)
    vec = pl.BlockSpec((1, D_MODEL), lambda i, j: (0, 0))
    cond = pl.BlockSpec((None, 1, D_MODEL), row)
    tile = pl.BlockSpec((tm, D_MODEL), lambda i, j: (i, 0))
    once = pl.BlockSpec((tm, D_MODEL), lambda i, j: (i, 0), pipeline_mode=pl.Buffered(1))
    in_specs = [
        tile,
        pl.BlockSpec((None, D_MODEL, th), lambda i, j: (layer, 0, j)),
        pl.BlockSpec((None, D_MODEL, th), lambda i, j: (layer, 0, nj + j)),
        pl.BlockSpec((None, th, D_MODEL), lambda i, j: (layer, j, 0)),
        once, vec, cond,
    ]
    args = [h, w_in, w_in, w_out, x, gpost, gate]
    scratch = []
    if nxt is not None:
        in_specs += [vec, cond, cond]
        args += list(nxt)
        out_specs = [once, once]
        out_shape = [jax.ShapeDtypeStruct((N_TOK, D_MODEL), F32),
                     jax.ShapeDtypeStruct((N_TOK, D_MODEL), BF16)]
    else:
        scratch = [pltpu.VMEM((tm, D_MODEL), F32)]
        out_specs = [
            pl.BlockSpec((tm, D_MODEL), lambda i, j: (jnp.minimum(i, npt - 1), 0)),
            pl.BlockSpec((tm, D_MODEL), lambda i, j: (jnp.maximum(i - npt, 0), 0)),
        ]
        out_shape = [jax.ShapeDtypeStruct((N_P, D_MODEL), F32),
                     jax.ShapeDtypeStruct((N_S, D_MODEL), F32)]
    return pl.pallas_call(
        functools.partial(_ffn_kernel, emit_next=nxt is not None, n_p_tiles=npt),
        grid=(N_TOK // tm, nj),
        in_specs=in_specs,
        out_specs=out_specs,
        out_shape=out_shape,
        scratch_shapes=scratch,
        compiler_params=_cparams(("arbitrary", "arbitrary"), 56),
        name="ffn",
    )(*args)


def kernel(x_prompt, x_sample, cache_k, cache_v, state_C, state_n, state_m, c, c_ctx, w_mod, b_mod, norm_pre1, norm_post1, norm_pre2, norm_post2, w_in, q_norm, k_norm, mlstm_gate_bias, mlstm_norm, conv_w, w_branch, w_out, w_ffn_in, w_ffn_out):
    cond = jnp.concatenate([c_ctx[None], c, jnp.zeros((N_COND - 1 - DEC_BATCH, D_MODEL), F32)], axis=0)
    mod = _modulation(cond, w_mod, b_mod)
    mod = mod.reshape(DEPTH, N_COND, 6, 1, D_MODEL).transpose(0, 2, 1, 3, 4)
    vec = lambda a: a.reshape(1, -1)
    cos, sin = _rope_tables()

    x, h = _prenorm(x_prompt.reshape(N_P, D_MODEL), x_sample.reshape(N_S, D_MODEL),
                    vec(norm_pre1[0]), mod[0, 1], mod[0, 0])

    ck = cache_k.reshape(DEC_BATCH, DEPTH, PAST_LEN, KV_WIDTH)
    cv = cache_v.reshape(DEC_BATCH, DEPTH, PAST_LEN, KV_WIDTH)
    st_n = state_n.reshape(DEC_BATCH, DEPTH, 2, M_HEADS, 1, M_DK)
    w_in_t = jnp.swapaxes(w_in, 1, 2)
    w_branch16 = w_branch.astype(BF16)
    w_out16 = w_out.astype(BF16)
    w_ffn_in16 = w_ffn_in.astype(BF16)
    w_ffn_out16 = w_ffn_out.astype(BF16)

    cache, state = None, None
    for l in range(DEPTH):
        sh1, sc1, g1, sh2, sc2, g2 = (mod[l, i] for i in range(6))
        gate_bias = jnp.pad(mlstm_gate_bias[l], (0, LANES - GATE_COLS)).reshape(1, LANES)
        mnorm_rep = jnp.broadcast_to(mlstm_norm[l][:, None], (M_WIDTH, LANES))

        p32, p16, pt16 = _projection(h, w_in_t, l)
        ld, rows = _gates(h, w_in_t, l, gate_bias)

        qw, kw = vec(q_norm[l]), vec(k_norm[l])
        att_p, new_k, new_v = _attention_ctx(p32, qw, kw, l, cache)
        cache = (new_k, new_v)
        att_s = _attention_dec(p32, ck, cv, qw, kw, cos, sin, l)

        ht_p, *state = _mlstm(p16, pt16, ld, rows, n_seq=BATCH, seq_len=SEQ, row0=0, layer=l,
                              prev_state=state, emit_state=True)
        m0 = jnp.broadcast_to(state_m[:, l].reshape(DEC_BATCH, 2, M_HEADS, 1, 1),
                              (DEC_BATCH, 2, M_HEADS, 1, LANES))
        (ht_s,) = _mlstm(p16, pt16, ld, rows, n_seq=DEC_BATCH, seq_len=DEC_SEQ, row0=N_P, layer=l,
                         init=(state_C, st_n, m0))

        merged = _merge(att_p, att_s, ht_p, ht_s, p16, pt16, w_branch16, l, mnorm_rep,
                        jnp.pad(conv_w[l], ((0, 5), (0, 0))))
        x, h2 = _outproj(merged, w_out16, l, x, vec(norm_post1[l]), g1,
                         vec(norm_pre2[l]), sc2, sh2)
        if l + 1 < DEPTH:
            nxt = (vec(norm_pre1[l + 1]), mod[l + 1, 1], mod[l + 1, 0])
            x, h = _ffn(h2, w_ffn_in16, w_ffn_out16, l, x, vec(norm_post2[l]), g2, nxt)
        else:
            y_p, y_s = _ffn(h2, w_ffn_in16, w_ffn_out16, l, x, vec(norm_post2[l]), g2)

    new_k, new_v = cache
    c_fin, n_fin, m_fin = state
    return (y_p.reshape(BATCH, SEQ, D_MODEL), y_s.reshape(DEC_BATCH, DEC_SEQ, D_MODEL),
            new_k.reshape(BATCH, DEPTH, SEQ, ATT_KV_HEADS, HEAD_DIM),
            new_v.reshape(BATCH, DEPTH, SEQ, ATT_KV_HEADS, HEAD_DIM),
            c_fin, n_fin.reshape(BATCH, DEPTH, 2, M_HEADS, M_DK), m_fin[:, :, :, :, 0, 0])
```

```python
import functools

import jax
import jax.numpy as jnp
from jax import lax
from jax.experimental import pallas as pl
from jax.experimental.pallas import tpu as pltpu

F32 = jnp.float32
BF16 = jnp.bfloat16

D_MODEL = 2048
BATCH = 16
SEQ = 256
DEPTH = 2
DEC_BATCH = 2
DEC_SEQ = 2048
PAST_LEN = 256
GRID_W = 64
EPS = 1e-6
HEAD_DIM = 128
ATT_Q_HEADS = 8
ATT_KV_HEADS = 2
ATT_GROUP = ATT_Q_HEADS // ATT_KV_HEADS
ATT_WIDTH = ATT_Q_HEADS * HEAD_DIM
KV_WIDTH = ATT_KV_HEADS * HEAD_DIM
ROPE_THETA = 10000.0
ROPE_FREQS = HEAD_DIM // 4
M_HEADS = 4
M_DK = 256
M_DV = 256
M_WIDTH = M_HEADS * M_DV
M_CHUNK = 128
CONV_WIDTH = 1024
N_BRANCH = 3
BRANCH_W = 1024
FF_HIDDEN = 5632
IN_WIDTH = 14864

LANES = 128
N_P = BATCH * SEQ
N_S = DEC_BATCH * DEC_SEQ
N_TOK = N_P + N_S
N_CHUNKS = N_TOK // M_CHUNK
N_COND = 8
KV_LEN_S = PAST_LEN + DEC_SEQ

W_AQ, W_AK, W_AV = 0, 1024, 1280
W_MQ, W_MV, W_MG, W_CB = 1536, 3584, 5632, 5648
GATE_COLS = 4 * M_HEADS
PROJ_TN = 512
N_TILE_A = W_MG // PROJ_TN
N_TILE_C = (IN_WIDTH - W_CB) // PROJ_TN
N_TILE_32 = W_MQ // PROJ_TN
TILE_T0 = W_MV // PROJ_TN
TILE_T1 = N_TILE_A

P32_W = W_MQ
C16_GL, C16_MQ, C16_MK = 0, 6144, 7168
C16_CB, C16_CC, C16_CX = 8192, 9216, 10240
P16_W = 11264
RT_MV, RT_MO = 0, 1024
PT_H = 2048
M_AUG = M_DV + LANES
GATE_ROWS = 24

NT_DIMS = (((1,), (1,)), ((), ()))
TN_DIMS = (((0,), (0,)), ((), ()))


def _cparams(semantics, vmem_mb):
    return pltpu.CompilerParams(dimension_semantics=semantics,
                                vmem_limit_bytes=vmem_mb * 1024 * 1024)


def _rms(x):
    return x * lax.rsqrt(jnp.mean(x * x, axis=-1, keepdims=True) + EPS)


def _cond_row(tile, tm):
    return jnp.where(tile < N_P // tm, 0, 1 + (tile * tm - N_P) // DEC_SEQ)


def _mod_kernel(c_ref, w_ref, b_ref, o_ref):
    c = c_ref[...]
    a = (c * jax.nn.sigmoid(c)).astype(BF16)
    o_ref[...] = jnp.dot(a, w_ref[...].astype(BF16), preferred_element_type=F32) + b_ref[...]


def _modulation(cond, w_mod, b_mod):
    tn = 1024
    n = 6 * D_MODEL
    return pl.pallas_call(
        _mod_kernel,
        grid=(DEPTH, n // tn),
        in_specs=[
            pl.BlockSpec((N_COND, D_MODEL), lambda l, j: (0, 0)),
            pl.BlockSpec((None, D_MODEL, tn), lambda l, j: (l, 0, j)),
            pl.BlockSpec((None, 1, tn), lambda l, j: (l, 0, j)),
        ],
        out_specs=pl.BlockSpec((None, N_COND, tn), lambda l, j: (l, 0, j)),
        out_shape=jax.ShapeDtypeStruct((DEPTH, N_COND, n), F32),
        compiler_params=_cparams(("parallel", "parallel"), 32),
        name="modulation",
    )(cond, w_mod, b_mod.reshape(DEPTH, 1, n))


def _prenorm_kernel(xp_ref, xs_ref, g_ref, sc_ref, sh_ref, h_ref, *, n_p_tiles):
    def emit(src_ref):
        y = _rms(src_ref[...]) * g_ref[...]
        h_ref[...] = (y * (1 + sc_ref[...]) + sh_ref[...]).astype(BF16)

    i = pl.program_id(0)
    pl.when(i < n_p_tiles)(lambda: emit(xp_ref))
    pl.when(i >= n_p_tiles)(lambda: emit(xs_ref))


def _prenorm(xp, xs, g, sc, sh):
    tm = 512
    npt = N_P // tm
    row = lambda i: (_cond_row(i, tm), 0, 0)
    tile = pl.BlockSpec((tm, D_MODEL), lambda i: (i, 0))
    return pl.pallas_call(
        functools.partial(_prenorm_kernel, n_p_tiles=npt),
        grid=(N_TOK // tm,),
        in_specs=[
            pl.BlockSpec((tm, D_MODEL), lambda i: (jnp.minimum(i, npt - 1), 0)),
            pl.BlockSpec((tm, D_MODEL), lambda i: (jnp.maximum(i - npt, 0), 0)),
            pl.BlockSpec((1, D_MODEL), lambda i: (0, 0)),
            pl.BlockSpec((None, 1, D_MODEL), row),
            pl.BlockSpec((None, 1, D_MODEL), row),
        ],
        out_specs=tile,
        out_shape=jax.ShapeDtypeStruct((N_TOK, D_MODEL), BF16),
        compiler_params=_cparams(("arbitrary",), 32),
        name="prenorm",
    )(xp, xs, g, sc, sh)


def _proj_tile16(j):
    first_gl = N_TILE_A + 3 * CONV_WIDTH // PROJ_TN
    mq0 = C16_MQ // PROJ_TN
    return jnp.where(j < TILE_T0, mq0 + jnp.maximum(j - N_TILE_32, 0),
                     jnp.where(j < TILE_T1, mq0 + TILE_T0 - N_TILE_32 - 1,
                               jnp.where(j < first_gl, C16_CB // PROJ_TN + (j - N_TILE_A),
                                         j - first_gl)))


FIN_ROWS, FOUT_ROWS = 32, 128
N_FIN_BLK = D_MODEL // FIN_ROWS
N_FOUT_BLK = FF_HIDDEN // FOUT_ROWS


def _proj_kernel(x_ref, w_ref, tail_ref, fin_ref, fout_ref,
                 o32_ref, o16_ref, ot_ref, fin16_ref, fout16_ref, w_scr):
    j = pl.program_id(1)
    step = pl.program_id(0) * pl.num_programs(1) + j

    @pl.when(step < N_FIN_BLK)
    def _():
        fin16_ref[...] = fin_ref[...].astype(BF16)

    @pl.when((step >= N_FIN_BLK) & (step < N_FIN_BLK + N_FOUT_BLK))
    def _():
        fout16_ref[...] = fout_ref[...].astype(BF16)

    shifted = j >= N_TILE_A
    off = pl.multiple_of(jnp.where(shifted, GATE_COLS, 0), 8)
    body = PROJ_TN - GATE_COLS
    w_scr[0:body, :] = w_ref[pl.ds(off, body), :].astype(BF16)
    w_scr[body:, :] = jnp.where(shifted, tail_ref[...], w_ref[body:, :]).astype(BF16)

    transposed = (j >= TILE_T0) & (j < TILE_T1)

    @pl.when(transposed)
    def _():
        acc_t = lax.dot_general(w_scr[...], x_ref[...], NT_DIMS, preferred_element_type=F32)
        ot_ref[...] = acc_t.astype(BF16)

    @pl.when(j < N_TILE_32)
    def _():
        o32_ref[...] = lax.dot_general(x_ref[...], w_scr[...], NT_DIMS, preferred_element_type=F32)

    @pl.when((j >= N_TILE_32) & jnp.logical_not(transposed))
    def _():
        acc = lax.dot_general(x_ref[...], w_scr[...], NT_DIMS, preferred_element_type=F32)
        o16_ref[...] = acc.astype(BF16)


def _projection(h, w_t, w_ffn_in, w_ffn_out, layer):
    tm = 2048
    nj = N_TILE_A + N_TILE_C
    tails_per_tile = PROJ_TN // GATE_COLS
    tail0 = W_MG // GATE_COLS
    assert (N_TOK // tm) * nj >= N_FIN_BLK + N_FOUT_BLK

    def tail_idx(i, j):
        return (layer, jnp.where(j < N_TILE_A, tail0, (j + 1) * tails_per_tile), 0)

    fin_blk = lambda i, j: jnp.minimum(i * nj + j, N_FIN_BLK - 1)
    fout_blk = lambda i, j: jnp.clip(i * nj + j - N_FIN_BLK, 0, N_FOUT_BLK - 1)
    return pl.pallas_call(
        _proj_kernel,
        grid=(N_TOK // tm, nj),
        in_specs=[
            pl.BlockSpec((tm, D_MODEL), lambda i, j: (i, 0), pipeline_mode=pl.Buffered(1)),
            pl.BlockSpec((None, PROJ_TN, D_MODEL), lambda i, j: (layer, j, 0)),
            pl.BlockSpec((None, GATE_COLS, D_MODEL), tail_idx),
            pl.BlockSpec((None, FIN_ROWS, 2 * FF_HIDDEN), lambda i, j: (layer, fin_blk(i, j), 0)),
            pl.BlockSpec((None, FOUT_ROWS, D_MODEL), lambda i, j: (layer, fout_blk(i, j), 0)),
        ],
        out_specs=[
            pl.BlockSpec((tm, PROJ_TN), lambda i, j: (i, jnp.minimum(j, N_TILE_32 - 1))),
            pl.BlockSpec((tm, PROJ_TN), lambda i, j: (i, _proj_tile16(j))),
            pl.BlockSpec((PROJ_TN, tm),
                         lambda i, j: (jnp.clip(j - TILE_T0, 0, TILE_T1 - TILE_T0 - 1), i)),
            pl.BlockSpec((FIN_ROWS, 2 * FF_HIDDEN), lambda i, j: (fin_blk(i, j), 0)),
            pl.BlockSpec((FOUT_ROWS, D_MODEL), lambda i, j: (fout_blk(i, j), 0)),
        ],
        out_shape=[jax.ShapeDtypeStruct((N_TOK, P32_W), F32),
                   jax.ShapeDtypeStruct((N_TOK, P16_W), BF16),
                   jax.ShapeDtypeStruct((PT_H, N_TOK), BF16),
                   jax.ShapeDtypeStruct((D_MODEL, 2 * FF_HIDDEN), BF16),
                   jax.ShapeDtypeStruct((FF_HIDDEN, D_MODEL), BF16)],
        scratch_shapes=[pltpu.VMEM((PROJ_TN, D_MODEL), BF16)],
        compiler_params=_cparams(("arbitrary", "arbitrary"), 56),
        name="projection",
    )(h, w_t, w_t, w_ffn_in, w_ffn_out)


def _gate_kernel(h_ref, w_ref, b_ref, ld_ref, row_ref, *, chunks):
    g = lax.dot_general(h_ref[...], w_ref[...].astype(BF16), NT_DIMS,
                        preferred_element_type=F32) + b_ref[...]
    lf = jax.nn.log_sigmoid(g)
    s_idx = lax.broadcasted_iota(jnp.int32, (M_CHUNK, M_CHUNK), 0)
    t_idx = lax.broadcasted_iota(jnp.int32, (M_CHUNK, M_CHUNK), 1)
    tril = (t_idx <= s_idx).astype(BF16)
    lane = lax.broadcasted_iota(jnp.int32, (M_CHUNK, LANES), 1)
    lf_hi = lf.astype(BF16)
    rest = lf - lf_hi.astype(F32)
    lf_mid = rest.astype(BF16)
    lf_lo = (rest - lf_mid.astype(F32)).astype(BF16)
    for ch in range(chunks):
        rows = slice(ch * M_CHUNK, (ch + 1) * M_CHUNK)
        gc = g[rows]
        lfc = lf[rows]
        pre = (jnp.dot(tril, lf_hi[rows], preferred_element_type=F32)
               + jnp.dot(tril, lf_mid[rows], preferred_element_type=F32)
               + jnp.dot(tril, lf_lo[rows], preferred_element_type=F32))
        tot = jnp.broadcast_to(pre[M_CHUNK - 1:M_CHUNK, :], pre.shape)
        suf = tot - pre + lfc
        both = jnp.where((lane >= 4) & (lane < 8), pre,
                         jnp.where((lane >= 12) & (lane < 16), suf, gc))
        both_t = both.T
        tot_t = tot.T
        for d in range(2):
            mask = (s_idx <= t_idx) if d == 0 else (s_idx >= t_idx)
            row_ref[d, ch, 20:GATE_ROWS, :] = jnp.zeros((GATE_ROWS - 20, M_CHUNK), F32)
            for hd in range(M_HEADS):
                li, lb = 8 * d + hd, 8 * d + 4 + hd
                i_row = both_t[li:li + 1, :]
                b_row = both_t[lb:lb + 1, :]
                bl_row = tot_t[lb:lb + 1, :]
                key = both[:, lb:lb + 1] - both[:, li:li + 1]
                ld = jnp.where(mask, b_row - key, -jnp.inf)
                wl = bl_row - b_row + i_row
                ld_ref[d, hd, rows, :] = ld
                row_ref[d, ch, hd:hd + 1, :] = i_row
                row_ref[d, ch, 4 + hd:5 + hd, :] = b_row
                row_ref[d, ch, 8 + hd:9 + hd, :] = jnp.max(ld, axis=0, keepdims=True)
                row_ref[d, ch, 12 + hd:13 + hd, :] = bl_row
                row_ref[d, ch, 16 + hd:17 + hd, :] = jnp.broadcast_to(
                    jnp.max(wl, axis=-1, keepdims=True), (1, M_CHUNK))


def _gates(h, w_t, layer, bias):
    tm = 512
    chunks = tm // M_CHUNK
    return pl.pallas_call(
        functools.partial(_gate_kernel, chunks=chunks),
        grid=(N_TOK // tm,),
        in_specs=[
            pl.BlockSpec((tm, D_MODEL), lambda i: (i, 0)),
            pl.BlockSpec((None, LANES, D_MODEL), lambda i: (layer, W_MG // LANES, 0)),
            pl.BlockSpec((1, LANES), lambda i: (0, 0)),
        ],
        out_specs=[
            pl.BlockSpec((2, M_HEADS, tm, M_CHUNK), lambda i: (0, 0, i, 0)),
            pl.BlockSpec((2, chunks, GATE_ROWS, M_CHUNK), lambda i: (0, i, 0, 0)),
        ],
        out_shape=[
            jax.ShapeDtypeStruct((2, M_HEADS, N_TOK, M_CHUNK), F32),
            jax.ShapeDtypeStruct((2, N_CHUNKS, GATE_ROWS, M_CHUNK), F32),
        ],
        compiler_params=_cparams(("parallel",), 32),
        name="mlstm_gates",
    )(h, w_t, bias)


def _mlstm_kernel(*refs, n_par, has_init, emit_state, has_prev):
    pos = 0
    seq_refs = []
    for _ in range(n_par):
        seq_refs.append(refs[pos:pos + 5])
        pos += 5
    if has_init:
        c0_ref, n0_ref, m0_ref = refs[pos:pos + 3]
        pos += 3
    if has_prev:
        pos += 3
    h_ref = refs[pos]
    pos += 1
    if emit_state:
        cout_ref, nout_ref, mout_ref = refs[pos:pos + 3]
        pos += 3
    c_scr, m_scr = refs[pos:pos + 2]

    ci = pl.program_id(2)
    last = pl.num_programs(2) - 1
    pad_rows = lax.broadcasted_iota(jnp.int32, (LANES, M_CHUNK), 0)
    one_row = (pad_rows == 0).astype(F32)

    @pl.when(ci == 0)
    def _():
        if has_init:
            for u in range(n_par):
                for hd in range(M_HEADS):
                    c_scr[u, hd, 0:M_DV, :] = c0_ref[u, hd].T
                    pad = lax.broadcasted_iota(jnp.int32, (LANES, M_DK), 0)
                    c_scr[u, hd, M_DV:M_AUG, :] = jnp.where(pad == 0, n0_ref[u, hd], 0.0)
            m_scr[...] = m0_ref[...]
        else:
            c_scr[...] = jnp.zeros_like(c_scr)
            m_scr[...] = jnp.zeros_like(m_scr)

    for u in range(n_par):
        q_ref, k_ref, vt_ref, ld_ref, row_ref = seq_refs[u]
        for hd in range(M_HEADS):
            cols = slice(hd * M_DK, (hd + 1) * M_DK)
            q = q_ref[:, cols]
            k = k_ref[:, cols] * (M_DK ** -0.5)
            vt_aug = jnp.concatenate([vt_ref[cols, :].astype(F32), one_row], axis=0)
            ld = ld_ref[hd]
            i_row = row_ref[hd:hd + 1, :]
            b_row = row_ref[4 + hd:5 + hd, :]
            ldmax_row = row_ref[8 + hd:9 + hd, :]
            bl_row = row_ref[12 + hd:13 + hd, :]
            wmax_row = row_ref[16 + hd:17 + hd, :]
            mem = c_scr[u, hd]
            m_prev = m_scr[u, hd]

            g_row = b_row + m_prev
            mt_row = jnp.maximum(g_row, ldmax_row)
            kq = lax.dot_general(k, q, NT_DIMS, preferred_element_type=F32)
            st = (kq * jnp.exp(ld - mt_row)).astype(BF16)
            nd = jnp.exp(g_row - mt_row) * lax.dot_general(
                mem.astype(BF16), q, NT_DIMS, preferred_element_type=F32) \
                + jnp.dot(vt_aug.astype(BF16), st, preferred_element_type=F32)
            den_row = nd[M_DV:M_DV + 1, :]
            scale_row = 1.0 / jnp.maximum(jnp.abs(den_row), jnp.exp(-mt_row))
            h_ref[u, cols, :] = nd[0:M_DV, :] * scale_row

            m_new = jnp.maximum(bl_row + m_prev, wmax_row)
            w_row = jnp.exp(bl_row - b_row + i_row - m_new)
            dec = jnp.exp(bl_row + m_prev - m_new)
            dec_wide = jnp.concatenate([dec] * (M_DK // LANES), axis=1)
            c_scr[u, hd] = dec_wide * mem + jnp.dot((vt_aug * w_row).astype(BF16), k,
                                                    preferred_element_type=F32)
            m_scr[u, hd] = m_new

    if emit_state:
        @pl.when(ci == last)
        def _():
            if has_prev:
                slot = lambda ref: ref
            else:
                slot = lambda ref: ref.at[:, 0]
                for ref in (cout_ref, nout_ref, mout_ref):
                    for l in range(1, DEPTH):
                        ref[:, l] = jnp.zeros(ref.shape[:1] + ref.shape[2:], F32)
            for u in range(n_par):
                for hd in range(M_HEADS):
                    slot(cout_ref)[u, hd] = c_scr[u, hd, 0:M_DV, :].T
                    slot(nout_ref)[u, hd] = c_scr[u, hd, M_DV:M_DV + 1, :]
            slot(mout_ref)[...] = m_scr[...]


def _mlstm(p16, pt16, ld, rows, *, n_seq, seq_len, row0, layer, init=None, prev_state=None,
           emit_state=False):
    n_par = 2
    nc = seq_len // M_CHUNK
    blk0 = row0 // M_CHUNK

    def chunk_of(c, d):
        return jnp.where(d == 0, c, nc - 1 - c)

    in_specs, args = [], []
    for u in range(n_par):
        blk = lambda g, d, c, u=u: blk0 + (g * n_par + u) * nc + chunk_of(c, d)
        in_specs += [
            pl.BlockSpec((M_CHUNK, M_WIDTH), lambda g, d, c, blk=blk: (blk(g, d, c), C16_MQ // M_WIDTH)),
            pl.BlockSpec((M_CHUNK, M_WIDTH), lambda g, d, c, blk=blk: (blk(g, d, c), C16_MK // M_WIDTH)),
            pl.BlockSpec((M_WIDTH, M_CHUNK), lambda g, d, c, blk=blk: (RT_MV // M_WIDTH, blk(g, d, c))),
            pl.BlockSpec((None, M_HEADS, M_CHUNK, M_CHUNK), lambda g, d, c, blk=blk: (d, 0, blk(g, d, c), 0)),
            pl.BlockSpec((None, None, GATE_ROWS, M_CHUNK), lambda g, d, c, blk=blk: (d, blk(g, d, c), 0, 0)),
        ]
        args += [p16, p16, pt16, ld, rows]
    if init is not None:
        in_specs += [
            pl.BlockSpec((n_par, None, None, M_HEADS, M_DK, M_DV), lambda g, d, c: (g, layer, d, 0, 0, 0)),
            pl.BlockSpec((n_par, None, None, M_HEADS, 1, M_DK), lambda g, d, c: (g, layer, d, 0, 0, 0)),
            pl.BlockSpec((n_par, None, M_HEADS, 1, LANES), lambda g, d, c: (g, d, 0, 0, 0)),
        ]
        args += list(init)
    aliases = {}
    if prev_state is not None:
        for k_, a in enumerate(prev_state):
            aliases[len(args)] = 1 + k_
            in_specs.append(pl.BlockSpec(memory_space=pl.ANY))
            args.append(a)
    out_specs = [pl.BlockSpec((None, n_par, None, M_WIDTH, M_CHUNK),
                              lambda g, d, c: (d, g, chunk_of(c, d), 0, 0))]
    out_shape = [jax.ShapeDtypeStruct((2, n_seq, nc, M_WIDTH, M_CHUNK), F32)]
    if emit_state:
        if prev_state is None:
            lblk, lidx = DEPTH, 0
        else:
            lblk, lidx = None, layer
        out_specs += [
            pl.BlockSpec((n_par, lblk, None, M_HEADS, M_DK, M_DV), lambda g, d, c: (g, lidx, d, 0, 0, 0)),
            pl.BlockSpec((n_par, lblk, None, M_HEADS, 1, M_DK), lambda g, d, c: (g, lidx, d, 0, 0, 0)),
            pl.BlockSpec((n_par, lblk, None, M_HEADS, 1, LANES), lambda g, d, c: (g, lidx, d, 0, 0, 0)),
        ]
        out_shape += [
            jax.ShapeDtypeStruct((n_seq, DEPTH, 2, M_HEADS, M_DK, M_DV), F32),
            jax.ShapeDtypeStruct((n_seq, DEPTH, 2, M_HEADS, 1, M_DK), F32),
            jax.ShapeDtypeStruct((n_seq, DEPTH, 2, M_HEADS, 1, LANES), F32),
        ]
    return pl.pallas_call(
        functools.partial(_mlstm_kernel, n_par=n_par, has_init=init is not None,
                          emit_state=emit_state, has_prev=prev_state is not None),
        grid=(n_seq // n_par, 2, nc),
        in_specs=in_specs,
        out_specs=out_specs,
        out_shape=out_shape,
        input_output_aliases=aliases,
        scratch_shapes=[
            pltpu.VMEM((n_par, M_HEADS, M_AUG, M_DK), F32),
            pltpu.VMEM((n_par, M_HEADS, 1, LANES), F32),
        ],
        compiler_params=_cparams(("arbitrary", "arbitrary", "arbitrary"), 40),
        name="mlstm_scan_dec" if init is not None else "mlstm_scan_ctx",
    )(*args)


def _swap32(x):
    lane = lax.broadcasted_iota(jnp.int32, x.shape, 1)
    return jnp.where((lane % 64) < 32, pltpu.roll(x, HEAD_DIM - 32, axis=1),
                     pltpu.roll(x, 32, axis=1))


def _group_attention(q_heads, kb, vb, o_ref):
    for g, q in enumerate(q_heads):
        s = lax.dot_general(q.astype(BF16), kb, NT_DIMS, preferred_element_type=F32)
        e = jnp.exp2(s - jnp.max(s, axis=-1, keepdims=True))
        l = jnp.sum(e, axis=-1, keepdims=True)
        o = jnp.dot(e.astype(BF16), vb, preferred_element_type=F32) / l
        o_ref[:, g * HEAD_DIM:(g + 1) * HEAD_DIM] = o.astype(BF16)


Q_SCALE = HEAD_DIM ** -0.5 * 1.4426950408889634


def _attn_ctx_kernel(*refs, first_layer):
    q_ref, k_ref, v_ref, qw_ref, kw_ref = refs[:5]
    o_ref, kout_ref, vout_ref = refs[-3:]
    k = _rms(k_ref[...]) * kw_ref[...]
    v = v_ref[...]
    if first_layer:
        kout_ref[0] = k
        vout_ref[0] = v
        for l in range(1, DEPTH):
            kout_ref[l] = jnp.zeros_like(k)
            vout_ref[l] = jnp.zeros_like(v)
    else:
        kout_ref[...] = k
        vout_ref[...] = v
    q_heads = [_rms(q_ref[:, g * HEAD_DIM:(g + 1) * HEAD_DIM]) * qw_ref[...] * Q_SCALE
               for g in range(ATT_GROUP)]
    _group_attention(q_heads, k.astype(BF16), v.astype(BF16), o_ref)


def _attention_ctx(p32, qw, kw, layer, prev_cache=None):
    gw = ATT_GROUP * HEAD_DIM
    in_specs = [
        pl.BlockSpec((SEQ, gw), lambda b, h: (b, W_AQ // gw + h)),
        pl.BlockSpec((SEQ, HEAD_DIM), lambda b, h: (b, W_AK // HEAD_DIM + h)),
        pl.BlockSpec((SEQ, HEAD_DIM), lambda b, h: (b, W_AV // HEAD_DIM + h)),
        pl.BlockSpec((1, HEAD_DIM), lambda b, h: (0, 0)),
        pl.BlockSpec((1, HEAD_DIM), lambda b, h: (0, 0)),
    ]
    args = [p32, p32, p32, qw, kw]
    aliases = {}
    if prev_cache is not None:
        for k_, a in enumerate(prev_cache):
            aliases[len(args)] = 1 + k_
            in_specs.append(pl.BlockSpec(memory_space=pl.ANY))
            args.append(a)
    if prev_cache is None:
        cache_spec = pl.BlockSpec((None, DEPTH, SEQ, HEAD_DIM), lambda b, h: (b, 0, 0, h))
    else:
        cache_spec = pl.BlockSpec((None, None, SEQ, HEAD_DIM), lambda b, h: (b, layer, 0, h))
    cache_shape = jax.ShapeDtypeStruct((BATCH, DEPTH, SEQ, KV_WIDTH), F32)
    return pl.pallas_call(
        functools.partial(_attn_ctx_kernel, first_layer=prev_cache is None),
        grid=(BATCH, ATT_KV_HEADS),
        in_specs=in_specs,
        out_specs=[pl.BlockSpec((SEQ, gw), lambda b, h: (b, h)), cache_spec, cache_spec],
        out_shape=[jax.ShapeDtypeStruct((N_P, ATT_WIDTH), BF16), cache_shape, cache_shape],
        input_output_aliases=aliases,
        compiler_params=_cparams(("arbitrary", "arbitrary"), 32),
        name="attention_ctx",
    )(*args)


def _attn_dec_kernel(q_ref, k_ref, v_ref, ck_ref, cv_ref, qw_ref, kw_ref,
                     cos_ref, sin_ref, cosq_ref, sinq_ref, o_ref, k_scr, v_scr):
    @pl.when(pl.program_id(2) == 0)
    def _():
        k_scr[0:PAST_LEN, :] = ck_ref[...].astype(BF16)
        v_scr[0:PAST_LEN, :] = cv_ref[...].astype(BF16)
        k = _rms(k_ref[...]) * kw_ref[...]
        k = k * cos_ref[...] + _swap32(k) * sin_ref[...]
        k_scr[PAST_LEN:, :] = k.astype(BF16)
        v_scr[PAST_LEN:, :] = v_ref[...].astype(BF16)

    q_heads = []
    for g in range(ATT_GROUP):
        q = _rms(q_ref[:, g * HEAD_DIM:(g + 1) * HEAD_DIM]) * qw_ref[...]
        q_heads.append((q * cosq_ref[...] + _swap32(q) * sinq_ref[...]) * Q_SCALE)
    _group_attention(q_heads, k_scr[...], v_scr[...], o_ref)


def _attention_dec(p32, ck, cv, qw, kw, cos, sin, layer):
    tq = 128
    gw = ATT_GROUP * HEAD_DIM
    nq = DEC_SEQ // tq
    seq0 = N_P // DEC_SEQ
    cache_spec = pl.BlockSpec((None, None, PAST_LEN, HEAD_DIM), lambda b, h, i: (b, layer, 0, h))
    return pl.pallas_call(
        _attn_dec_kernel,
        grid=(DEC_BATCH, ATT_KV_HEADS, nq),
        in_specs=[
            pl.BlockSpec((tq, gw), lambda b, h, i: (N_P // tq + b * nq + i, W_AQ // gw + h)),
            pl.BlockSpec((DEC_SEQ, HEAD_DIM), lambda b, h, i: (seq0 + b, W_AK // HEAD_DIM + h)),
            pl.BlockSpec((DEC_SEQ, HEAD_DIM), lambda b, h, i: (seq0 + b, W_AV // HEAD_DIM + h)),
            cache_spec, cache_spec,
            pl.BlockSpec((1, HEAD_DIM), lambda b, h, i: (0, 0)),
            pl.BlockSpec((1, HEAD_DIM), lambda b, h, i: (0, 0)),
            pl.BlockSpec((DEC_SEQ, HEAD_DIM), lambda b, h, i: (0, 0)),
            pl.BlockSpec((DEC_SEQ, HEAD_DIM), lambda b, h, i: (0, 0)),
            pl.BlockSpec((tq, HEAD_DIM), lambda b, h, i: (i, 0)),
            pl.BlockSpec((tq, HEAD_DIM), lambda b, h, i: (i, 0)),
        ],
        out_specs=pl.BlockSpec((tq, gw), lambda b, h, i: (b * nq + i, h)),
        out_shape=jax.ShapeDtypeStruct((N_S, ATT_WIDTH), BF16),
        scratch_shapes=[
            pltpu.VMEM((KV_LEN_S, HEAD_DIM), BF16),
            pltpu.VMEM((KV_LEN_S, HEAD_DIM), BF16),
        ],
        compiler_params=_cparams(("arbitrary", "arbitrary", "arbitrary"), 32),
        name="attention_dec",
    )(p32, p32, p32, ck, cv, qw, kw, cos, sin, cos, sin)


def _rope_tables():
    t = jnp.arange(DEC_SEQ)
    inv = ROPE_THETA ** (-jnp.arange(ROPE_FREQS, dtype=F32) / ROPE_FREQS)
    ang = jnp.stack([t // GRID_W, t % GRID_W], axis=-1).astype(F32)[:, :, None] * inv
    cos = jnp.cos(ang)
    sin = jnp.sin(ang)
    cos = jnp.stack([cos, cos], axis=2).reshape(DEC_SEQ, HEAD_DIM)
    sin = jnp.stack([-sin, sin], axis=2).reshape(DEC_SEQ, HEAD_DIM)
    return cos, sin


def _merge_kernel(attp_ref, atts_ref, hfp_ref, hbp_ref, hfs_ref, hbs_ref,
                  mot_ref, cb_ref, cc_ref, cx_ref, ccp_ref, cxp_ref, ccn_ref, cxn_ref,
                  gl0_ref, gl1_ref, gl2_ref, wb_ref, mn_ref, cw_ref, o_ref, *, tm, halo):
    i = pl.program_id(0)
    row0 = i * tm
    in_p = row0 < N_P
    off = jnp.where(in_p, row0 % SEQ, (row0 - N_P) % DEC_SEQ)
    seq_len = jnp.where(in_p, SEQ, DEC_SEQ)
    first = off == 0
    final = off + tm == seq_len

    hsum = jnp.where(in_p, hfp_ref[...] + hbp_ref[...], hfs_ref[...] + hbs_ref[...])
    mn = mn_ref[...]
    chunk_parts = []
    for ch in range(tm // M_CHUNK):
        parts = []
        for hd in range(M_HEADS):
            x = hsum[ch, hd * M_DV:(hd + 1) * M_DV, :]
            parts.append(x * lax.rsqrt(jnp.mean(x * x, axis=0, keepdims=True) + EPS))
        chunk_parts.append(jnp.concatenate(parts, axis=0) * mn)
    ml_t = jnp.concatenate(chunk_parts, axis=1) * jax.nn.sigmoid(mot_ref[...].astype(F32))

    u = cc_ref[...].astype(F32) * cx_ref[...].astype(F32)
    u_halo_prev = ccp_ref[...].astype(F32) * cxp_ref[...].astype(F32)
    u_halo_next = ccn_ref[...].astype(F32) * cxn_ref[...].astype(F32)
    u_prev_row = jnp.where(first, 0.0, u_halo_prev[halo - 1:halo, :])
    u_next_row = jnp.where(final, 0.0, u_halo_next[0:1, :])
    ridx = lax.broadcasted_iota(jnp.int32, u.shape, 0)
    u_prev = jnp.where(ridx == 0, u_prev_row, pltpu.roll(u, 1, axis=0))
    u_next = jnp.where(ridx == tm - 1, u_next_row, pltpu.roll(u, tm - 1, axis=0))
    cw = cw_ref[...]
    cv = cb_ref[...].astype(F32) * (cw[0:1, :] * u_prev + cw[1:2, :] * u + cw[2:3, :] * u_next)

    att = jnp.where(in_p, attp_ref[...], atts_ref[...])
    acc = jax.nn.sigmoid(gl0_ref[...].astype(F32)) * jnp.dot(att, wb_ref[0],
                                                             preferred_element_type=F32)
    acc += jax.nn.sigmoid(gl1_ref[...].astype(F32)) * lax.dot_general(
        ml_t.astype(BF16), wb_ref[1], TN_DIMS, preferred_element_type=F32)
    acc += jax.nn.sigmoid(gl2_ref[...].astype(F32)) * jnp.dot(cv.astype(BF16), wb_ref[2],
                                                              preferred_element_type=F32)
    o_ref[...] = acc.astype(BF16)


def _merge(att_p, att_s, ht_p, ht_s, p16, pt16, wb, layer, mnorm_rep, convw):
    tm = 2 * M_CHUNK
    cpt = tm // M_CHUNK
    halo = 16
    nt = N_TOK // tm
    npt = N_P // tm
    th = tm // halo
    last_h = N_TOK // halo - 1
    p_idx = lambda i: jnp.minimum(i, npt - 1)
    s_idx = lambda i: jnp.maximum(i - npt, 0)
    cblk = lambda col: (lambda i: (i, col // BRANCH_W))
    prevh = lambda col: (lambda i: (jnp.maximum(i * th - 1, 0), col // BRANCH_W))
    nexth = lambda col: (lambda i: (jnp.minimum((i + 1) * th, last_h), col // BRANCH_W))
    glblk = lambda g: (lambda i: (i, C16_GL // D_MODEL + g))
    ht_p = ht_p.reshape(2, N_P // M_CHUNK, M_WIDTH, M_CHUNK)
    ht_s = ht_s.reshape(2, N_S // M_CHUNK, M_WIDTH, M_CHUNK)
    ht_spec = lambda d, idx: pl.BlockSpec((None, cpt, M_WIDTH, M_CHUNK), lambda i: (d, idx(i), 0, 0))
    return pl.pallas_call(
        functools.partial(_merge_kernel, tm=tm, halo=halo),
        grid=(nt,),
        in_specs=[
            pl.BlockSpec((tm, BRANCH_W), lambda i: (p_idx(i), 0)),
            pl.BlockSpec((tm, BRANCH_W), lambda i: (s_idx(i), 0)),
            ht_spec(0, p_idx), ht_spec(1, p_idx), ht_spec(0, s_idx), ht_spec(1, s_idx),
            pl.BlockSpec((M_WIDTH, tm), lambda i: (RT_MO // M_WIDTH, i)),
            pl.BlockSpec((tm, BRANCH_W), cblk(C16_CB)),
            pl.BlockSpec((tm, BRANCH_W), cblk(C16_CC)),
            pl.BlockSpec((tm, BRANCH_W), cblk(C16_CX)),
            pl.BlockSpec((halo, BRANCH_W), prevh(C16_CC)),
            pl.BlockSpec((halo, BRANCH_W), prevh(C16_CX)),
            pl.BlockSpec((halo, BRANCH_W), nexth(C16_CC)),
            pl.BlockSpec((halo, BRANCH_W), nexth(C16_CX)),
            pl.BlockSpec((tm, D_MODEL), glblk(0)),
            pl.BlockSpec((tm, D_MODEL), glblk(1)),
            pl.BlockSpec((tm, D_MODEL), glblk(2)),
            pl.BlockSpec((None, N_BRANCH, BRANCH_W, D_MODEL), lambda i: (layer, 0, 0, 0),
                         pipeline_mode=pl.Buffered(1)),
            pl.BlockSpec((M_WIDTH, LANES), lambda i: (0, 0)),
            pl.BlockSpec((8, CONV_WIDTH), lambda i: (0, 0)),
        ],
        out_specs=pl.BlockSpec((tm, D_MODEL), lambda i: (i, 0)),
        out_shape=jax.ShapeDtypeStruct((N_TOK, D_MODEL), BF16),
        compiler_params=_cparams(("arbitrary",), 52),
        name="branch_merge",
    )(att_p, att_s, ht_p, ht_p, ht_s, ht_s, pt16, p16, p16, p16, p16, p16, p16, p16,
      p16, p16, p16, wb, mnorm_rep, convw)


def _outproj_kernel(*refs, split_x, n_p_tiles):
    m_ref, w_ref = refs[:2]
    if split_x:
        xp_ref, xs_ref = refs[2:4]
        rest = refs[4:]
    else:
        x_ref = refs[2]
        rest = refs[3:]
    gpost_ref, gate_ref, gpre_ref, sc_ref, sh_ref, x_out_ref, h_out_ref = rest
    if split_x:
        x_in = jnp.where(pl.program_id(0) < n_p_tiles, xp_ref[...], xs_ref[...])
    else:
        x_in = x_ref[...]
    mix = jnp.dot(m_ref[...], w_ref[...], preferred_element_type=F32)
    x = x_in + gate_ref[...] * (_rms(mix) * gpost_ref[...])
    x_out_ref[...] = x
    h = (_rms(x) * gpre_ref[...]) * (1 + sc_ref[...]) + sh_ref[...]
    h_out_ref[...] = h.astype(BF16)


def _outproj(merged, w_out, layer, x, gpost, gate, gpre, sc, sh):
    tm = 512
    npt = N_P // tm
    row = lambda i: (_cond_row(i, tm), 0, 0)
    vec = pl.BlockSpec((1, D_MODEL), lambda i: (0, 0))
    cond = pl.BlockSpec((None, 1, D_MODEL), row)
    tile = pl.BlockSpec((tm, D_MODEL), lambda i: (i, 0))
    split_x = isinstance(x, tuple)
    if split_x:
        x_specs = [pl.BlockSpec((tm, D_MODEL), lambda i: (jnp.minimum(i, npt - 1), 0)),
                   pl.BlockSpec((tm, D_MODEL), lambda i: (jnp.maximum(i - npt, 0), 0))]
        x_args = list(x)
    else:
        x_specs, x_args = [tile], [x]
    return pl.pallas_call(
        functools.partial(_outproj_kernel, split_x=split_x, n_p_tiles=npt),
        grid=(N_TOK // tm,),
        in_specs=[tile,
                  pl.BlockSpec((None, D_MODEL, D_MODEL), lambda i: (layer, 0, 0),
                               pipeline_mode=pl.Buffered(1)),
                  *x_specs, vec, cond, vec, cond, cond],
        out_specs=[tile, tile],
        out_shape=[jax.ShapeDtypeStruct((N_TOK, D_MODEL), F32),
                   jax.ShapeDtypeStruct((N_TOK, D_MODEL), BF16)],
        compiler_params=_cparams(("arbitrary",), 48),
        name="out_projection",
    )(merged, w_out, *x_args, gpost, gate, gpre, sc, sh)


def _ffn_kernel(*refs, emit_next, n_p_tiles):
    h_ref, wg_ref, wu_ref, wo_ref, x_ref, gpost_ref, gate_ref = refs[:7]
    if emit_next:
        gpre_ref, sc_ref, sh_ref, x_out_ref, h_out_ref, acc_ref = refs[7:]
    else:
        yp_ref, ys_ref, acc_ref = refs[7:]

    i = pl.program_id(0)
    j = pl.program_id(1)

    @pl.when(j == 0)
    def _():
        acc_ref[...] = jnp.zeros_like(acc_ref)

    h = h_ref[...]
    gate = jnp.dot(h, wg_ref[...], preferred_element_type=F32)
    up = jnp.dot(h, wu_ref[...], preferred_element_type=F32)
    act = (gate * jax.nn.sigmoid(gate) * up).astype(BF16)
    acc_ref[...] += jnp.dot(act, wo_ref[...], preferred_element_type=F32)

    def result():
        return x_ref[...] + gate_ref[...] * (_rms(acc_ref[...]) * gpost_ref[...])

    is_last = j == pl.num_programs(1) - 1
    if emit_next:
        @pl.when(is_last)
        def _():
            x = result()
            x_out_ref[...] = x
            hn = (_rms(x) * gpre_ref[...]) * (1 + sc_ref[...]) + sh_ref[...]
            h_out_ref[...] = hn.astype(BF16)
    else:
        @pl.when(is_last & (i < n_p_tiles))
        def _():
            yp_ref[...] = result()

        @pl.when(is_last & (i >= n_p_tiles))
        def _():
            ys_ref[...] = result()


def _ffn(h, w_in, w_out, x, gpost, gate, nxt=None):
    tm, th = 512, 512
    nj = FF_HIDDEN // th
    npt = N_P // tm
    row = lambda i, j: (_cond_row(i, tm), 0, 0)
    vec = pl.BlockSpec((1, D_MODEL), lambda i, j: (0, 0))
    cond = pl.BlockSpec((None, 1, D_MODEL), row)
    tile = pl.BlockSpec((tm, D_MODEL), lambda i, j: (i, 0))
    in_specs = [
        tile,
        pl.BlockSpec((D_MODEL, th), lambda i, j: (0, j)),
        pl.BlockSpec((D_MODEL, th), lambda i, j: (0, nj + j)),
        pl.BlockSpec((th, D_MODEL), lambda i, j: (j, 0)),
        tile, vec, cond,
    ]
    args = [h, w_in, w_in, w_out, x, gpost, gate]
    if nxt is not None:
        in_specs += [vec, cond, cond]
        args += list(nxt)
        out_specs = [tile, tile]
        out_shape = [jax.ShapeDtypeStruct((N_TOK, D_MODEL), F32),
                     jax.ShapeDtypeStruct((N_TOK, D_MODEL), BF16)]
    else:
        out_specs = [
            pl.BlockSpec((tm, D_MODEL), lambda i, j: (jnp.minimum(i, npt - 1), 0)),
            pl.BlockSpec((tm, D_MODEL), lambda i, j: (jnp.maximum(i - npt, 0), 0)),
        ]
        out_shape = [jax.ShapeDtypeStruct((N_P, D_MODEL), F32),
                     jax.ShapeDtypeStruct((N_S, D_MODEL), F32)]
    return pl.pallas_call(
        functools.partial(_ffn_kernel, emit_next=nxt is not None, n_p_tiles=npt),
        grid=(N_TOK // tm, nj),
        in_specs=in_specs,
        out_specs=out_specs,
        out_shape=out_shape,
        scratch_shapes=[pltpu.VMEM((tm, D_MODEL), F32)],
        compiler_params=_cparams(("arbitrary", "arbitrary"), 56),
        name="ffn",
    )(*args)


def kernel(x_prompt, x_sample, cache_k, cache_v, state_C, state_n, state_m, c, c_ctx, w_mod, b_mod, norm_pre1, norm_post1, norm_pre2, norm_post2, w_in, q_norm, k_norm, mlstm_gate_bias, mlstm_norm, conv_w, w_branch, w_out, w_ffn_in, w_ffn_out):
    cond = jnp.concatenate([c_ctx[None], c, jnp.zeros((N_COND - 1 - DEC_BATCH, D_MODEL), F32)], axis=0)
    mod = _modulation(cond, w_mod, b_mod)
    mod = mod.reshape(DEPTH, N_COND, 6, 1, D_MODEL).transpose(0, 2, 1, 3, 4)
    vec = lambda a: a.reshape(1, -1)
    cos, sin = _rope_tables()

    x = (x_prompt.reshape(N_P, D_MODEL), x_sample.reshape(N_S, D_MODEL))
    h = _prenorm(*x, vec(norm_pre1[0]), mod[0, 1], mod[0, 0])

    ck = cache_k.reshape(DEC_BATCH, DEPTH, PAST_LEN, KV_WIDTH)
    cv = cache_v.reshape(DEC_BATCH, DEPTH, PAST_LEN, KV_WIDTH)
    st_n = state_n.reshape(DEC_BATCH, DEPTH, 2, M_HEADS, 1, M_DK)
    w_in_t = jnp.swapaxes(w_in, 1, 2)
    w_branch16 = w_branch.astype(BF16)
    w_out16 = w_out.astype(BF16)

    cache, state = None, None
    for l in range(DEPTH):
        sh1, sc1, g1, sh2, sc2, g2 = (mod[l, i] for i in range(6))
        gate_bias = jnp.pad(mlstm_gate_bias[l], (0, LANES - GATE_COLS)).reshape(1, LANES)
        mnorm_rep = jnp.broadcast_to(mlstm_norm[l][:, None], (M_WIDTH, LANES))

        p32, p16, pt16, w_ffn_in16, w_ffn_out16 = _projection(h, w_in_t, w_ffn_in, w_ffn_out, l)
        ld, rows = _gates(h, w_in_t, l, gate_bias)

        qw, kw = vec(q_norm[l]), vec(k_norm[l])
        att_p, new_k, new_v = _attention_ctx(p32, qw, kw, l, cache)
        cache = (new_k, new_v)
        att_s = _attention_dec(p32, ck, cv, qw, kw, cos, sin, l)

        ht_p, *state = _mlstm(p16, pt16, ld, rows, n_seq=BATCH, seq_len=SEQ, row0=0, layer=l,
                              prev_state=state, emit_state=True)
        m0 = jnp.broadcast_to(state_m[:, l].reshape(DEC_BATCH, 2, M_HEADS, 1, 1),
                              (DEC_BATCH, 2, M_HEADS, 1, LANES))
        (ht_s,) = _mlstm(p16, pt16, ld, rows, n_seq=DEC_BATCH, seq_len=DEC_SEQ, row0=N_P, layer=l,
                         init=(state_C, st_n, m0))

        merged = _merge(att_p, att_s, ht_p, ht_s, p16, pt16, w_branch16, l, mnorm_rep,
                        jnp.pad(conv_w[l], ((0, 5), (0, 0))))
        x, h2 = _outproj(merged, w_out16, l, x, vec(norm_post1[l]), g1,
                         vec(norm_pre2[l]), sc2, sh2)
        if l + 1 < DEPTH:
            nxt = (vec(norm_pre1[l + 1]), mod[l + 1, 1], mod[l + 1, 0])
            x, h = _ffn(h2, w_ffn_in16, w_ffn_out16, x, vec(norm_post2[l]), g2, nxt)
        else:
            y_p, y_s = _ffn(h2, w_ffn_in16, w_ffn_out16, x, vec(norm_post2[l]), g2)

    new_k, new_v = cache
    c_fin, n_fin, m_fin = state
    return (y_p.reshape(BATCH, SEQ, D_MODEL), y_s.reshape(DEC_BATCH, DEC_SEQ, D_MODEL),
            new_k.reshape(BATCH, DEPTH, SEQ, ATT_KV_HEADS, HEAD_DIM),
            new_v.reshape(BATCH, DEPTH, SEQ, ATT_KV_HEADS, HEAD_DIM),
            c_fin, n_fin.reshape(BATCH, DEPTH, 2, M_HEADS, M_DK), m_fin[:, :, :, :, 0, 0])
```

```python
import functools

import jax
import jax.numpy as jnp
from jax import lax
from jax.experimental import pallas as pl
from jax.experimental.pallas import tpu as pltpu

F32 = jnp.float32
BF16 = jnp.bfloat16

D_MODEL = 2048
BATCH = 16
SEQ = 256
DEPTH = 2
DEC_BATCH = 2
DEC_SEQ = 2048
PAST_LEN = 256
GRID_W = 64
EPS = 1e-6
HEAD_DIM = 128
ATT_Q_HEADS = 8
ATT_KV_HEADS = 2
ATT_GROUP = ATT_Q_HEADS // ATT_KV_HEADS
ATT_WIDTH = ATT_Q_HEADS * HEAD_DIM
KV_WIDTH = ATT_KV_HEADS * HEAD_DIM
ROPE_THETA = 10000.0
ROPE_FREQS = HEAD_DIM // 4
M_HEADS = 4
M_DK = 256
M_DV = 256
M_WIDTH = M_HEADS * M_DV
M_CHUNK = 128
CONV_WIDTH = 1024
N_BRANCH = 3
BRANCH_W = 1024
FF_HIDDEN = 5632
IN_WIDTH = 14864

LANES = 128
N_P = BATCH * SEQ
N_S = DEC_BATCH * DEC_SEQ
N_TOK = N_P + N_S
N_CHUNKS = N_TOK // M_CHUNK
N_COND = 8
KV_LEN_S = PAST_LEN + DEC_SEQ

W_AQ, W_AK, W_AV = 0, 1024, 1280
W_MQ, W_MV, W_MG, W_CB = 1536, 3584, 5632, 5648
GATE_COLS = 4 * M_HEADS
PROJ_TN = 512
N_TILE_A = W_MG // PROJ_TN
N_TILE_C = (IN_WIDTH - W_CB) // PROJ_TN
N_TILE_32 = W_MQ // PROJ_TN
TILE_T0 = W_MV // PROJ_TN
TILE_T1 = N_TILE_A
TILE_MO = TILE_T0 + M_WIDTH // PROJ_TN
TILE_GL = N_TILE_A + 3 * CONV_WIDTH // PROJ_TN

P32_W = W_MQ
C16_GL, C16_MQ, C16_MK = 0, 6144, 7168
C16_CB, C16_CC, C16_CX = 8192, 9216, 10240
P16_W = 11264
RT_MV, RT_MO = 0, 1024
PT_H = 2048
M_AUG = M_DV + LANES
GATE_ROWS = 24

NT_DIMS = (((1,), (1,)), ((), ()))
TN_DIMS = (((0,), (0,)), ((), ()))


def _cparams(semantics, vmem_mb):
    return pltpu.CompilerParams(dimension_semantics=semantics,
                                vmem_limit_bytes=vmem_mb * 1024 * 1024)


def _rms(x):
    return x * lax.rsqrt(jnp.mean(x * x, axis=-1, keepdims=True) + EPS)


def _twice_sigmoid_of_twice(x_half):
    return jnp.tanh(x_half) + 1.0


def _sigmoid(x):
    return 0.5 * _twice_sigmoid_of_twice(0.5 * x)


def _cond_row(tile, tm):
    return jnp.where(tile < N_P // tm, 0, 1 + (tile * tm - N_P) // DEC_SEQ)


def _mod_kernel(c_ref, w_ref, b_ref, o_ref):
    c = c_ref[...]
    a = (c * _sigmoid(c)).astype(BF16)
    o_ref[...] = jnp.dot(a, w_ref[...].astype(BF16), preferred_element_type=F32) + b_ref[...]


def _modulation(cond, w_mod, b_mod):
    tn = 1024
    n = 6 * D_MODEL
    return pl.pallas_call(
        _mod_kernel,
        grid=(DEPTH, n // tn),
        in_specs=[
            pl.BlockSpec((N_COND, D_MODEL), lambda l, j: (0, 0)),
            pl.BlockSpec((None, D_MODEL, tn), lambda l, j: (l, 0, j)),
            pl.BlockSpec((None, 1, tn), lambda l, j: (l, 0, j)),
        ],
        out_specs=pl.BlockSpec((None, N_COND, tn), lambda l, j: (l, 0, j)),
        out_shape=jax.ShapeDtypeStruct((DEPTH, N_COND, n), F32),
        compiler_params=_cparams(("parallel", "parallel"), 32),
        name="modulation",
    )(cond, w_mod, b_mod.reshape(DEPTH, 1, n))


def _prenorm_kernel(xp_ref, xs_ref, g_ref, sc_ref, sh_ref, h_ref, *, n_p_tiles):
    def emit(src_ref):
        y = _rms(src_ref[...]) * g_ref[...]
        h_ref[...] = (y * (1 + sc_ref[...]) + sh_ref[...]).astype(BF16)

    i = pl.program_id(0)
    pl.when(i < n_p_tiles)(lambda: emit(xp_ref))
    pl.when(i >= n_p_tiles)(lambda: emit(xs_ref))


def _prenorm(xp, xs, g, sc, sh):
    tm = 512
    npt = N_P // tm
    row = lambda i: (_cond_row(i, tm), 0, 0)
    tile = pl.BlockSpec((tm, D_MODEL), lambda i: (i, 0))
    return pl.pallas_call(
        functools.partial(_prenorm_kernel, n_p_tiles=npt),
        grid=(N_TOK // tm,),
        in_specs=[
            pl.BlockSpec((tm, D_MODEL), lambda i: (jnp.minimum(i, npt - 1), 0)),
            pl.BlockSpec((tm, D_MODEL), lambda i: (jnp.maximum(i - npt, 0), 0)),
            pl.BlockSpec((1, D_MODEL), lambda i: (0, 0)),
            pl.BlockSpec((None, 1, D_MODEL), row),
            pl.BlockSpec((None, 1, D_MODEL), row),
        ],
        out_specs=tile,
        out_shape=jax.ShapeDtypeStruct((N_TOK, D_MODEL), BF16),
        compiler_params=_cparams(("arbitrary",), 32),
        name="prenorm",
    )(xp, xs, g, sc, sh)


def _proj_tile16(j):
    first_gl = N_TILE_A + 3 * CONV_WIDTH // PROJ_TN
    mq0 = C16_MQ // PROJ_TN
    return jnp.where(j < TILE_T0, mq0 + jnp.maximum(j - N_TILE_32, 0),
                     jnp.where(j < TILE_T1, mq0 + TILE_T0 - N_TILE_32 - 1,
                               jnp.where(j < first_gl, C16_CB // PROJ_TN + (j - N_TILE_A),
                                         j - first_gl)))


FIN_ROWS, FOUT_ROWS = 32, 128
N_FIN_BLK = D_MODEL // FIN_ROWS
N_FOUT_BLK = FF_HIDDEN // FOUT_ROWS


def _proj_kernel(x_ref, w_ref, tail_ref, fin_ref, fout_ref,
                 o32_ref, o16_ref, ot_ref, fin16_ref, fout16_ref, w_scr):
    j = pl.program_id(1)
    shifted = j >= N_TILE_A
    off = pl.multiple_of(jnp.where(shifted, GATE_COLS, 0), 8)
    body = PROJ_TN - GATE_COLS
    halved = ((j >= TILE_MO) & (j < TILE_T1)) | (j >= TILE_GL)
    scale = jnp.where(halved, 0.5, 1.0)
    transposed = (j >= TILE_T0) & (j < TILE_T1)

    def prepare():
        w_scr[0:body, :] = (w_ref[pl.ds(off, body), :] * scale).astype(BF16)
        w_scr[body:, :] = (jnp.where(shifted, tail_ref[...], w_ref[body:, :]) * scale).astype(BF16)
        fin16_ref[...] = fin_ref[...].astype(BF16)
        fout16_ref[...] = fout_ref[...].astype(BF16)

    @pl.when(transposed)
    def _():
        prepare()
        acc_t = lax.dot_general(w_scr[...], x_ref[...], NT_DIMS, preferred_element_type=F32)
        ot_ref[...] = acc_t.astype(BF16)

    @pl.when(j < N_TILE_32)
    def _():
        prepare()
        o32_ref[...] = lax.dot_general(x_ref[...], w_scr[...], NT_DIMS, preferred_element_type=F32)

    @pl.when((j >= N_TILE_32) & jnp.logical_not(transposed))
    def _():
        prepare()
        acc = lax.dot_general(x_ref[...], w_scr[...], NT_DIMS, preferred_element_type=F32)
        o16_ref[...] = acc.astype(BF16)


def _projection(h, w_t, w_ffn_in, w_ffn_out, layer):
    tm = 2048
    nj = N_TILE_A + N_TILE_C
    tails_per_tile = PROJ_TN // GATE_COLS
    tail0 = W_MG // GATE_COLS
    assert (N_TOK // tm) * nj >= N_FIN_BLK + N_FOUT_BLK

    def tail_idx(i, j):
        return (layer, jnp.where(j < N_TILE_A, tail0, (j + 1) * tails_per_tile), 0)

    fin_blk = lambda i, j: jnp.minimum(i * nj + j, N_FIN_BLK - 1)
    fout_blk = lambda i, j: jnp.clip(i * nj + j - N_FIN_BLK, 0, N_FOUT_BLK - 1)
    return pl.pallas_call(
        _proj_kernel,
        grid=(N_TOK // tm, nj),
        in_specs=[
            pl.BlockSpec((tm, D_MODEL), lambda i, j: (i, 0), pipeline_mode=pl.Buffered(1)),
            pl.BlockSpec((None, PROJ_TN, D_MODEL), lambda i, j: (layer, j, 0)),
            pl.BlockSpec((None, GATE_COLS, D_MODEL), tail_idx),
            pl.BlockSpec((None, FIN_ROWS, 2 * FF_HIDDEN), lambda i, j: (layer, fin_blk(i, j), 0)),
            pl.BlockSpec((None, FOUT_ROWS, D_MODEL), lambda i, j: (layer, fout_blk(i, j), 0)),
        ],
        out_specs=[
            pl.BlockSpec((tm, PROJ_TN), lambda i, j: (i, jnp.minimum(j, N_TILE_32 - 1))),
            pl.BlockSpec((tm, PROJ_TN), lambda i, j: (i, _proj_tile16(j))),
            pl.BlockSpec((PROJ_TN, tm),
                         lambda i, j: (jnp.clip(j - TILE_T0, 0, TILE_T1 - TILE_T0 - 1), i)),
            pl.BlockSpec((FIN_ROWS, 2 * FF_HIDDEN), lambda i, j: (fin_blk(i, j), 0)),
            pl.BlockSpec((FOUT_ROWS, D_MODEL), lambda i, j: (fout_blk(i, j), 0)),
        ],
        out_shape=[jax.ShapeDtypeStruct((N_TOK, P32_W), F32),
                   jax.ShapeDtypeStruct((N_TOK, P16_W), BF16),
                   jax.ShapeDtypeStruct((PT_H, N_TOK), BF16),
                   jax.ShapeDtypeStruct((D_MODEL, 2 * FF_HIDDEN), BF16),
                   jax.ShapeDtypeStruct((FF_HIDDEN, D_MODEL), BF16)],
        scratch_shapes=[pltpu.VMEM((PROJ_TN, D_MODEL), BF16)],
        compiler_params=_cparams(("arbitrary", "arbitrary"), 56),
        name="projection",
    )(h, w_t, w_t, w_ffn_in, w_ffn_out)


def _gate_kernel(h_ref, w_ref, b_ref, ld_ref, row_ref, *, chunks):
    g = lax.dot_general(h_ref[...], w_ref[...].astype(BF16), NT_DIMS,
                        preferred_element_type=F32) + b_ref[...]
    lf = jax.nn.log_sigmoid(g)
    s_idx = lax.broadcasted_iota(jnp.int32, (M_CHUNK, M_CHUNK), 0)
    t_idx = lax.broadcasted_iota(jnp.int32, (M_CHUNK, M_CHUNK), 1)
    tril = (t_idx <= s_idx).astype(BF16)
    lane = lax.broadcasted_iota(jnp.int32, (M_CHUNK, LANES), 1)
    lf_hi = lf.astype(BF16)
    rest = lf - lf_hi.astype(F32)
    lf_mid = rest.astype(BF16)
    lf_lo = (rest - lf_mid.astype(F32)).astype(BF16)
    for ch in range(chunks):
        rows = slice(ch * M_CHUNK, (ch + 1) * M_CHUNK)
        gc = g[rows]
        lfc = lf[rows]
        pre = (jnp.dot(tril, lf_hi[rows], preferred_element_type=F32)
               + jnp.dot(tril, lf_mid[rows], preferred_element_type=F32)
               + jnp.dot(tril, lf_lo[rows], preferred_element_type=F32))
        tot = jnp.broadcast_to(pre[M_CHUNK - 1:M_CHUNK, :], pre.shape)
        suf = tot - pre + lfc
        both = jnp.where((lane >= 4) & (lane < 8), pre,
                         jnp.where((lane >= 12) & (lane < 16), suf, gc))
        both_t = both.T
        tot_t = tot.T
        for d in range(2):
            mask = (s_idx <= t_idx) if d == 0 else (s_idx >= t_idx)
            row_ref[d, ch, 20:GATE_ROWS, :] = jnp.zeros((GATE_ROWS - 20, M_CHUNK), F32)
            for hd in range(M_HEADS):
                li, lb = 8 * d + hd, 8 * d + 4 + hd
                i_row = both_t[li:li + 1, :]
                b_row = both_t[lb:lb + 1, :]
                bl_row = tot_t[lb:lb + 1, :]
                key = both[:, lb:lb + 1] - both[:, li:li + 1]
                ld = jnp.where(mask, b_row - key, -jnp.inf)
                wl = bl_row - b_row + i_row
                ld_ref[d, hd, rows, :] = ld
                row_ref[d, ch, hd:hd + 1, :] = i_row
                row_ref[d, ch, 4 + hd:5 + hd, :] = b_row
                row_ref[d, ch, 8 + hd:9 + hd, :] = jnp.max(ld, axis=0, keepdims=True)
                row_ref[d, ch, 12 + hd:13 + hd, :] = bl_row
                row_ref[d, ch, 16 + hd:17 + hd, :] = jnp.broadcast_to(
                    jnp.max(wl, axis=-1, keepdims=True), (1, M_CHUNK))


def _gates(h, w_t, layer, bias):
    tm = 512
    chunks = tm // M_CHUNK
    return pl.pallas_call(
        functools.partial(_gate_kernel, chunks=chunks),
        grid=(N_TOK // tm,),
        in_specs=[
            pl.BlockSpec((tm, D_MODEL), lambda i: (i, 0)),
            pl.BlockSpec((None, LANES, D_MODEL), lambda i: (layer, W_MG // LANES, 0)),
            pl.BlockSpec((1, LANES), lambda i: (0, 0)),
        ],
        out_specs=[
            pl.BlockSpec((2, M_HEADS, tm, M_CHUNK), lambda i: (0, 0, i, 0)),
            pl.BlockSpec((2, chunks, GATE_ROWS, M_CHUNK), lambda i: (0, i, 0, 0)),
        ],
        out_shape=[
            jax.ShapeDtypeStruct((2, M_HEADS, N_TOK, M_CHUNK), F32),
            jax.ShapeDtypeStruct((2, N_CHUNKS, GATE_ROWS, M_CHUNK), F32),
        ],
        compiler_params=_cparams(("parallel",), 32),
        name="mlstm_gates",
    )(h, w_t, bias)


def _mlstm_kernel(*refs, n_par, has_init, emit_state, has_prev):
    pos = 0
    seq_refs = []
    for _ in range(n_par):
        seq_refs.append(refs[pos:pos + 5])
        pos += 5
    if has_init:
        c0_ref, n0_ref, m0_ref = refs[pos:pos + 3]
        pos += 3
    if has_prev:
        pos += 3
    h_ref = refs[pos]
    pos += 1
    if emit_state:
        cout_ref, nout_ref, mout_ref = refs[pos:pos + 3]
        pos += 3
    c_scr, m_scr = refs[pos:pos + 2]

    ci = pl.program_id(2)
    last = pl.num_programs(2) - 1
    pad_rows = lax.broadcasted_iota(jnp.int32, (LANES, M_CHUNK), 0)
    one_row = (pad_rows == 0).astype(F32)

    @pl.when(ci == 0)
    def _():
        if has_init:
            for u in range(n_par):
                for hd in range(M_HEADS):
                    c_scr[u, hd, 0:M_DV, :] = c0_ref[u, hd].T
                    pad = lax.broadcasted_iota(jnp.int32, (LANES, M_DK), 0)
                    c_scr[u, hd, M_DV:M_AUG, :] = jnp.where(pad == 0, n0_ref[u, hd], 0.0)
            m_scr[...] = m0_ref[...]
        else:
            c_scr[...] = jnp.zeros_like(c_scr)
            m_scr[...] = jnp.zeros_like(m_scr)

    for u in range(n_par):
        q_ref, k_ref, vt_ref, ld_ref, row_ref = seq_refs[u]
        for hd in range(M_HEADS):
            cols = slice(hd * M_DK, (hd + 1) * M_DK)
            q = q_ref[:, cols]
            k = k_ref[:, cols] * (M_DK ** -0.5)
            vt_aug = jnp.concatenate([vt_ref[cols, :].astype(F32), one_row], axis=0)
            ld = ld_ref[hd]
            i_row = row_ref[hd:hd + 1, :]
            b_row = row_ref[4 + hd:5 + hd, :]
            ldmax_row = row_ref[8 + hd:9 + hd, :]
            bl_row = row_ref[12 + hd:13 + hd, :]
            wmax_row = row_ref[16 + hd:17 + hd, :]
            mem = c_scr[u, hd]
            m_prev = m_scr[u, hd]

            g_row = b_row + m_prev
            mt_row = jnp.maximum(g_row, ldmax_row)
            kq = lax.dot_general(k, q, NT_DIMS, preferred_element_type=F32)
            st = (kq * jnp.exp(ld - mt_row)).astype(BF16)
            nd = jnp.exp(g_row - mt_row) * lax.dot_general(
                mem.astype(BF16), q, NT_DIMS, preferred_element_type=F32) \
                + jnp.dot(vt_aug.astype(BF16), st, preferred_element_type=F32)
            den_row = nd[M_DV:M_DV + 1, :]
            scale_row = 1.0 / jnp.maximum(jnp.abs(den_row), jnp.exp(-mt_row))
            h_ref[u, cols, :] = nd[0:M_DV, :] * scale_row

            m_new = jnp.maximum(bl_row + m_prev, wmax_row)
            w_row = jnp.exp(bl_row - b_row + i_row - m_new)
            dec = jnp.exp(bl_row + m_prev - m_new)
            dec_wide = jnp.concatenate([dec] * (M_DK // LANES), axis=1)
            c_scr[u, hd] = dec_wide * mem + jnp.dot((vt_aug * w_row).astype(BF16), k,
                                                    preferred_element_type=F32)
            m_scr[u, hd] = m_new

    if emit_state:
        @pl.when(ci == last)
        def _():
            if has_prev:
                slot = lambda ref: ref
            else:
                slot = lambda ref: ref.at[:, 0]
                for ref in (cout_ref, nout_ref, mout_ref):
                    for l in range(1, DEPTH):
                        ref[:, l] = jnp.zeros(ref.shape[:1] + ref.shape[2:], F32)
            for u in range(n_par):
                for hd in range(M_HEADS):
                    slot(cout_ref)[u, hd] = c_scr[u, hd, 0:M_DV, :].T
                    slot(nout_ref)[u, hd] = c_scr[u, hd, M_DV:M_DV + 1, :]
            slot(mout_ref)[...] = m_scr[...]


def _mlstm(p16, pt16, ld, rows, *, n_seq, seq_len, row0, layer, init=None, prev_state=None,
           emit_state=False):
    n_par = 2
    nc = seq_len // M_CHUNK
    blk0 = row0 // M_CHUNK

    def chunk_of(c, d):
        return jnp.where(d == 0, c, nc - 1 - c)

    in_specs, args = [], []
    for u in range(n_par):
        blk = lambda g, d, c, u=u: blk0 + (g * n_par + u) * nc + chunk_of(c, d)
        in_specs += [
            pl.BlockSpec((M_CHUNK, M_WIDTH), lambda g, d, c, blk=blk: (blk(g, d, c), C16_MQ // M_WIDTH)),
            pl.BlockSpec((M_CHUNK, M_WIDTH), lambda g, d, c, blk=blk: (blk(g, d, c), C16_MK // M_WIDTH)),
            pl.BlockSpec((M_WIDTH, M_CHUNK), lambda g, d, c, blk=blk: (RT_MV // M_WIDTH, blk(g, d, c))),
            pl.BlockSpec((None, M_HEADS, M_CHUNK, M_CHUNK), lambda g, d, c, blk=blk: (d, 0, blk(g, d, c), 0)),
            pl.BlockSpec((None, None, GATE_ROWS, M_CHUNK), lambda g, d, c, blk=blk: (d, blk(g, d, c), 0, 0)),
        ]
        args += [p16, p16, pt16, ld, rows]
    if init is not None:
        in_specs += [
            pl.BlockSpec((n_par, None, None, M_HEADS, M_DK, M_DV), lambda g, d, c: (g, layer, d, 0, 0, 0)),
            pl.BlockSpec((n_par, None, None, M_HEADS, 1, M_DK), lambda g, d, c: (g, layer, d, 0, 0, 0)),
            pl.BlockSpec((n_par, None, M_HEADS, 1, LANES), lambda g, d, c: (g, d, 0, 0, 0)),
        ]
        args += list(init)
    aliases = {}
    if prev_state is not None:
        for k_, a in enumerate(prev_state):
            aliases[len(args)] = 1 + k_
            in_specs.append(pl.BlockSpec(memory_space=pl.ANY))
            args.append(a)
    out_specs = [pl.BlockSpec((None, n_par, None, M_WIDTH, M_CHUNK),
                              lambda g, d, c: (d, g, chunk_of(c, d), 0, 0))]
    out_shape = [jax.ShapeDtypeStruct((2, n_seq, nc, M_WIDTH, M_CHUNK), F32)]
    if emit_state:
        if prev_state is None:
            lblk, lidx = DEPTH, 0
        else:
            lblk, lidx = None, layer
        out_specs += [
            pl.BlockSpec((n_par, lblk, None, M_HEADS, M_DK, M_DV), lambda g, d, c: (g, lidx, d, 0, 0, 0)),
            pl.BlockSpec((n_par, lblk, None, M_HEADS, 1, M_DK), lambda g, d, c: (g, lidx, d, 0, 0, 0)),
            pl.BlockSpec((n_par, lblk, None, M_HEADS, 1, LANES), lambda g, d, c: (g, lidx, d, 0, 0, 0)),
        ]
        out_shape += [
            jax.ShapeDtypeStruct((n_seq, DEPTH, 2, M_HEADS, M_DK, M_DV), F32),
            jax.ShapeDtypeStruct((n_seq, DEPTH, 2, M_HEADS, 1, M_DK), F32),
            jax.ShapeDtypeStruct((n_seq, DEPTH, 2, M_HEADS, 1, LANES), F32),
        ]
    return pl.pallas_call(
        functools.partial(_mlstm_kernel, n_par=n_par, has_init=init is not None,
                          emit_state=emit_state, has_prev=prev_state is not None),
        grid=(n_seq // n_par, 2, nc),
        in_specs=in_specs,
        out_specs=out_specs,
        out_shape=out_shape,
        input_output_aliases=aliases,
        scratch_shapes=[
            pltpu.VMEM((n_par, M_HEADS, M_AUG, M_DK), F32),
            pltpu.VMEM((n_par, M_HEADS, 1, LANES), F32),
        ],
        compiler_params=_cparams(("arbitrary", "arbitrary", "arbitrary"), 40),
        name="mlstm_scan_dec" if init is not None else "mlstm_scan_ctx",
    )(*args)


def _swap32(x):
    lane = lax.broadcasted_iota(jnp.int32, x.shape, 1)
    return jnp.where((lane % 64) < 32, pltpu.roll(x, HEAD_DIM - 32, axis=1),
                     pltpu.roll(x, 32, axis=1))


def _group_attention(q_heads, kb, vb, o_ref):
    for g, q in enumerate(q_heads):
        s = lax.dot_general(q.astype(BF16), kb, NT_DIMS, preferred_element_type=F32)
        e = jnp.exp2(s - jnp.max(s, axis=-1, keepdims=True))
        l = jnp.sum(e, axis=-1, keepdims=True)
        o = jnp.dot(e.astype(BF16), vb, preferred_element_type=F32) / l
        o_ref[:, g * HEAD_DIM:(g + 1) * HEAD_DIM] = o.astype(BF16)


Q_SCALE = HEAD_DIM ** -0.5 * 1.4426950408889634


def _attn_ctx_kernel(*refs, first_layer):
    q_ref, k_ref, v_ref, qw_ref, kw_ref = refs[:5]
    o_ref, kout_ref, vout_ref = refs[-3:]
    k = _rms(k_ref[...]) * kw_ref[...]
    v = v_ref[...]
    if first_layer:
        kout_ref[0] = k
        vout_ref[0] = v
        for l in range(1, DEPTH):
            kout_ref[l] = jnp.zeros_like(k)
            vout_ref[l] = jnp.zeros_like(v)
    else:
        kout_ref[...] = k
        vout_ref[...] = v
    q_heads = [_rms(q_ref[:, g * HEAD_DIM:(g + 1) * HEAD_DIM]) * qw_ref[...] * Q_SCALE
               for g in range(ATT_GROUP)]
    _group_attention(q_heads, k.astype(BF16), v.astype(BF16), o_ref)


def _attention_ctx(p32, qw, kw, layer, prev_cache=None):
    gw = ATT_GROUP * HEAD_DIM
    in_specs = [
        pl.BlockSpec((SEQ, gw), lambda b, h: (b, W_AQ // gw + h)),
        pl.BlockSpec((SEQ, HEAD_DIM), lambda b, h: (b, W_AK // HEAD_DIM + h)),
        pl.BlockSpec((SEQ, HEAD_DIM), lambda b, h: (b, W_AV // HEAD_DIM + h)),
        pl.BlockSpec((1, HEAD_DIM), lambda b, h: (0, 0)),
        pl.BlockSpec((1, HEAD_DIM), lambda b, h: (0, 0)),
    ]
    args = [p32, p32, p32, qw, kw]
    aliases = {}
    if prev_cache is not None:
        for k_, a in enumerate(prev_cache):
            aliases[len(args)] = 1 + k_
            in_specs.append(pl.BlockSpec(memory_space=pl.ANY))
            args.append(a)
    if prev_cache is None:
        cache_spec = pl.BlockSpec((None, DEPTH, SEQ, HEAD_DIM), lambda b, h: (b, 0, 0, h))
    else:
        cache_spec = pl.BlockSpec((None, None, SEQ, HEAD_DIM), lambda b, h: (b, layer, 0, h))
    cache_shape = jax.ShapeDtypeStruct((BATCH, DEPTH, SEQ, KV_WIDTH), F32)
    return pl.pallas_call(
        functools.partial(_attn_ctx_kernel, first_layer=prev_cache is None),
        grid=(BATCH, ATT_KV_HEADS),
        in_specs=in_specs,
        out_specs=[pl.BlockSpec((SEQ, gw), lambda b, h: (b, h)), cache_spec, cache_spec],
        out_shape=[jax.ShapeDtypeStruct((N_P, ATT_WIDTH), BF16), cache_shape, cache_shape],
        input_output_aliases=aliases,
        compiler_params=_cparams(("arbitrary", "arbitrary"), 32),
        name="attention_ctx",
    )(*args)


def _attn_dec_kernel(q_ref, k_ref, v_ref, ck_ref, cv_ref, qw_ref, kw_ref,
                     cos_ref, sin_ref, cosq_ref, sinq_ref, o_ref, k_scr, v_scr):
    @pl.when(pl.program_id(2) == 0)
    def _():
        k_scr[0:PAST_LEN, :] = ck_ref[...].astype(BF16)
        v_scr[0:PAST_LEN, :] = cv_ref[...].astype(BF16)
        k = _rms(k_ref[...]) * kw_ref[...]
        k = k * cos_ref[...] + _swap32(k) * sin_ref[...]
        k_scr[PAST_LEN:, :] = k.astype(BF16)
        v_scr[PAST_LEN:, :] = v_ref[...].astype(BF16)

    q_heads = []
    for g in range(ATT_GROUP):
        q = _rms(q_ref[:, g * HEAD_DIM:(g + 1) * HEAD_DIM]) * qw_ref[...]
        q_heads.append((q * cosq_ref[...] + _swap32(q) * sinq_ref[...]) * Q_SCALE)
    _group_attention(q_heads, k_scr[...], v_scr[...], o_ref)


def _attention_dec(p32, ck, cv, qw, kw, cos, sin, layer):
    tq = 128
    gw = ATT_GROUP * HEAD_DIM
    nq = DEC_SEQ // tq
    seq0 = N_P // DEC_SEQ
    cache_spec = pl.BlockSpec((None, None, PAST_LEN, HEAD_DIM), lambda b, h, i: (b, layer, 0, h))
    return pl.pallas_call(
        _attn_dec_kernel,
        grid=(DEC_BATCH, ATT_KV_HEADS, nq),
        in_specs=[
            pl.BlockSpec((tq, gw), lambda b, h, i: (N_P // tq + b * nq + i, W_AQ // gw + h)),
            pl.BlockSpec((DEC_SEQ, HEAD_DIM), lambda b, h, i: (seq0 + b, W_AK // HEAD_DIM + h)),
            pl.BlockSpec((DEC_SEQ, HEAD_DIM), lambda b, h, i: (seq0 + b, W_AV // HEAD_DIM + h)),
            cache_spec, cache_spec,
            pl.BlockSpec((1, HEAD_DIM), lambda b, h, i: (0, 0)),
            pl.BlockSpec((1, HEAD_DIM), lambda b, h, i: (0, 0)),
            pl.BlockSpec((DEC_SEQ, HEAD_DIM), lambda b, h, i: (0, 0)),
            pl.BlockSpec((DEC_SEQ, HEAD_DIM), lambda b, h, i: (0, 0)),
            pl.BlockSpec((tq, HEAD_DIM), lambda b, h, i: (i, 0)),
            pl.BlockSpec((tq, HEAD_DIM), lambda b, h, i: (i, 0)),
        ],
        out_specs=pl.BlockSpec((tq, gw), lambda b, h, i: (b * nq + i, h)),
        out_shape=jax.ShapeDtypeStruct((N_S, ATT_WIDTH), BF16),
        scratch_shapes=[
            pltpu.VMEM((KV_LEN_S, HEAD_DIM), BF16),
            pltpu.VMEM((KV_LEN_S, HEAD_DIM), BF16),
        ],
        compiler_params=_cparams(("arbitrary", "arbitrary", "arbitrary"), 32),
        name="attention_dec",
    )(p32, p32, p32, ck, cv, qw, kw, cos, sin, cos, sin)


def _rope_tables():
    t = jnp.arange(DEC_SEQ)
    inv = ROPE_THETA ** (-jnp.arange(ROPE_FREQS, dtype=F32) / ROPE_FREQS)
    ang = jnp.stack([t // GRID_W, t % GRID_W], axis=-1).astype(F32)[:, :, None] * inv
    cos = jnp.cos(ang)
    sin = jnp.sin(ang)
    cos = jnp.stack([cos, cos], axis=2).reshape(DEC_SEQ, HEAD_DIM)
    sin = jnp.stack([-sin, sin], axis=2).reshape(DEC_SEQ, HEAD_DIM)
    return cos, sin


def _merge_kernel(attp_ref, atts_ref, hfp_ref, hbp_ref, hfs_ref, hbs_ref,
                  mot_ref, cb_ref, cc_ref, cx_ref, ccp_ref, cxp_ref, ccn_ref, cxn_ref,
                  gl0_ref, gl1_ref, gl2_ref, wb_ref, mn_ref, cw_ref, o_ref, att_scr, hsum_scr,
                  *, tm, halo):
    i = pl.program_id(0)
    row0 = i * tm
    in_p = row0 < N_P
    off = jnp.where(in_p, row0 % SEQ, (row0 - N_P) % DEC_SEQ)
    seq_len = jnp.where(in_p, SEQ, DEC_SEQ)
    first = off == 0
    final = off + tm == seq_len

    @pl.when(in_p)
    def _():
        att_scr[...] = attp_ref[...]
        hsum_scr[...] = hfp_ref[...] + hbp_ref[...]

    @pl.when(jnp.logical_not(in_p))
    def _():
        att_scr[...] = atts_ref[...]
        hsum_scr[...] = hfs_ref[...] + hbs_ref[...]

    mn_half = 0.5 * mn_ref[...]
    chunk_parts = []
    for ch in range(tm // M_CHUNK):
        parts = []
        for hd in range(M_HEADS):
            x = hsum_scr[ch, hd * M_DV:(hd + 1) * M_DV, :]
            parts.append(x * lax.rsqrt(jnp.mean(x * x, axis=0, keepdims=True) + EPS))
        chunk_parts.append(jnp.concatenate(parts, axis=0) * mn_half)
    ml_t = jnp.concatenate(chunk_parts, axis=1) * _twice_sigmoid_of_twice(
        mot_ref[...].astype(F32))

    u = cc_ref[...].astype(F32) * cx_ref[...].astype(F32)
    u_halo_prev = ccp_ref[...].astype(F32) * cxp_ref[...].astype(F32)
    u_halo_next = ccn_ref[...].astype(F32) * cxn_ref[...].astype(F32)
    u_prev_row = jnp.where(first, 0.0, u_halo_prev[halo - 1:halo, :])
    u_next_row = jnp.where(final, 0.0, u_halo_next[0:1, :])
    ridx = lax.broadcasted_iota(jnp.int32, u.shape, 0)
    u_prev = jnp.where(ridx == 0, u_prev_row, pltpu.roll(u, 1, axis=0))
    u_next = jnp.where(ridx == tm - 1, u_next_row, pltpu.roll(u, tm - 1, axis=0))
    cw = cw_ref[...]
    cv = cb_ref[...].astype(F32) * (cw[0:1, :] * u_prev + cw[1:2, :] * u + cw[2:3, :] * u_next)

    acc = _twice_sigmoid_of_twice(gl0_ref[...].astype(F32)) * jnp.dot(
        att_scr[...], wb_ref[0], preferred_element_type=F32)
    acc += _twice_sigmoid_of_twice(gl1_ref[...].astype(F32)) * lax.dot_general(
        ml_t.astype(BF16), wb_ref[1], TN_DIMS, preferred_element_type=F32)
    acc += _twice_sigmoid_of_twice(gl2_ref[...].astype(F32)) * jnp.dot(
        cv.astype(BF16), wb_ref[2], preferred_element_type=F32)
    o_ref[...] = (0.5 * acc).astype(BF16)


def _merge(att_p, att_s, ht_p, ht_s, p16, pt16, wb, layer, mnorm_rep, convw):
    tm = 2 * M_CHUNK
    cpt = tm // M_CHUNK
    halo = 16
    nt = N_TOK // tm
    npt = N_P // tm
    th = tm // halo
    last_h = N_TOK // halo - 1
    p_idx = lambda i: jnp.minimum(i, npt - 1)
    s_idx = lambda i: jnp.maximum(i - npt, 0)
    cblk = lambda col: (lambda i: (i, col // BRANCH_W))
    prevh = lambda col: (lambda i: (jnp.maximum(i * th - 1, 0), col // BRANCH_W))
    nexth = lambda col: (lambda i: (jnp.minimum((i + 1) * th, last_h), col // BRANCH_W))
    glblk = lambda g: (lambda i: (i, C16_GL // D_MODEL + g))
    ht_p = ht_p.reshape(2, N_P // M_CHUNK, M_WIDTH, M_CHUNK)
    ht_s = ht_s.reshape(2, N_S // M_CHUNK, M_WIDTH, M_CHUNK)
    ht_spec = lambda d, idx: pl.BlockSpec((None, cpt, M_WIDTH, M_CHUNK), lambda i: (d, idx(i), 0, 0))
    return pl.pallas_call(
        functools.partial(_merge_kernel, tm=tm, halo=halo),
        grid=(nt,),
        in_specs=[
            pl.BlockSpec((tm, BRANCH_W), lambda i: (p_idx(i), 0)),
            pl.BlockSpec((tm, BRANCH_W), lambda i: (s_idx(i), 0)),
            ht_spec(0, p_idx), ht_spec(1, p_idx), ht_spec(0, s_idx), ht_spec(1, s_idx),
            pl.BlockSpec((M_WIDTH, tm), lambda i: (RT_MO // M_WIDTH, i)),
            pl.BlockSpec((tm, BRANCH_W), cblk(C16_CB)),
            pl.BlockSpec((tm, BRANCH_W), cblk(C16_CC)),
            pl.BlockSpec((tm, BRANCH_W), cblk(C16_CX)),
            pl.BlockSpec((halo, BRANCH_W), prevh(C16_CC)),
            pl.BlockSpec((halo, BRANCH_W), prevh(C16_CX)),
            pl.BlockSpec((halo, BRANCH_W), nexth(C16_CC)),
            pl.BlockSpec((halo, BRANCH_W), nexth(C16_CX)),
            pl.BlockSpec((tm, D_MODEL), glblk(0)),
            pl.BlockSpec((tm, D_MODEL), glblk(1)),
            pl.BlockSpec((tm, D_MODEL), glblk(2)),
            pl.BlockSpec((None, N_BRANCH, BRANCH_W, D_MODEL), lambda i: (layer, 0, 0, 0),
                         pipeline_mode=pl.Buffered(1)),
            pl.BlockSpec((M_WIDTH, LANES), lambda i: (0, 0)),
            pl.BlockSpec((8, CONV_WIDTH), lambda i: (0, 0)),
        ],
        out_specs=pl.BlockSpec((tm, D_MODEL), lambda i: (i, 0)),
        out_shape=jax.ShapeDtypeStruct((N_TOK, D_MODEL), BF16),
        scratch_shapes=[pltpu.VMEM((tm, BRANCH_W), BF16),
                        pltpu.VMEM((cpt, M_WIDTH, M_CHUNK), F32)],
        compiler_params=_cparams(("arbitrary",), 52),
        name="branch_merge",
    )(att_p, att_s, ht_p, ht_p, ht_s, ht_s, pt16, p16, p16, p16, p16, p16, p16, p16,
      p16, p16, p16, wb, mnorm_rep, convw)


def _outproj_kernel(*refs, split_x, n_p_tiles):
    m_ref, w_ref = refs[:2]
    if split_x:
        xp_ref, xs_ref = refs[2:4]
        rest = refs[4:]
    else:
        x_ref = refs[2]
        rest = refs[3:]
    gpost_ref, gate_ref, gpre_ref, sc_ref, sh_ref, x_out_ref, h_out_ref = rest
    in_p = pl.program_id(0) < n_p_tiles
    post_scale = gate_ref[...] * gpost_ref[...]
    pre_scale = gpre_ref[...] * (1 + sc_ref[...])
    group = 128
    for r in range(m_ref.shape[0] // group):
        rows = slice(r * group, (r + 1) * group)
        if split_x:
            x_in = jnp.where(in_p, xp_ref[rows, :], xs_ref[rows, :])
        else:
            x_in = x_ref[rows, :]
        mix = jnp.dot(m_ref[rows, :], w_ref[...], preferred_element_type=F32)
        x = x_in + _rms(mix) * post_scale
        x_out_ref[rows, :] = x
        h_out_ref[rows, :] = (_rms(x) * pre_scale + sh_ref[...]).astype(BF16)


def _outproj(merged, w_out, layer, x, gpost, gate, gpre, sc, sh):
    tm = 512
    npt = N_P // tm
    row = lambda i: (_cond_row(i, tm), 0, 0)
    vec = pl.BlockSpec((1, D_MODEL), lambda i: (0, 0))
    cond = pl.BlockSpec((None, 1, D_MODEL), row)
    tile = pl.BlockSpec((tm, D_MODEL), lambda i: (i, 0))
    split_x = isinstance(x, tuple)
    if split_x:
        x_specs = [pl.BlockSpec((tm, D_MODEL), lambda i: (jnp.minimum(i, npt - 1), 0)),
                   pl.BlockSpec((tm, D_MODEL), lambda i: (jnp.maximum(i - npt, 0), 0))]
        x_args = list(x)
    else:
        x_specs, x_args = [tile], [x]
    return pl.pallas_call(
        functools.partial(_outproj_kernel, split_x=split_x, n_p_tiles=npt),
        grid=(N_TOK // tm,),
        in_specs=[tile,
                  pl.BlockSpec((None, D_MODEL, D_MODEL), lambda i: (layer, 0, 0),
                               pipeline_mode=pl.Buffered(1)),
                  *x_specs, vec, cond, vec, cond, cond],
        out_specs=[tile, tile],
        out_shape=[jax.ShapeDtypeStruct((N_TOK, D_MODEL), F32),
                   jax.ShapeDtypeStruct((N_TOK, D_MODEL), BF16)],
        compiler_params=_cparams(("arbitrary",), 48),
        name="out_projection",
    )(merged, w_out, *x_args, gpost, gate, gpre, sc, sh)


def _ffn_kernel(*refs, emit_next, n_p_tiles):
    h_ref, wg_ref, wu_ref, wo_ref, x_ref, gpost_ref, gate_ref = refs[:7]
    if emit_next:
        gpre_ref, sc_ref, sh_ref, x_out_ref, h_out_ref, acc_ref = refs[7:]
    else:
        yp_ref, ys_ref, acc_ref = refs[7:]

    i = pl.program_id(0)
    j = pl.program_id(1)

    @pl.when(j == 0)
    def _():
        acc_ref[...] = jnp.zeros_like(acc_ref)

    h = h_ref[...]
    gate = jnp.dot(h, wg_ref[...], preferred_element_type=F32)
    up = jnp.dot(h, wu_ref[...], preferred_element_type=F32)
    gate_half = 0.5 * gate
    act = (gate_half * _twice_sigmoid_of_twice(gate_half) * up).astype(BF16)
    acc_ref[...] += jnp.dot(act, wo_ref[...], preferred_element_type=F32)

    group = 64

    def result(rows, post_scale):
        return x_ref[rows, :] + _rms(acc_ref[rows, :]) * post_scale

    def epilogue(emit):
        post_scale = gate_ref[...] * gpost_ref[...]
        for r in range(acc_ref.shape[0] // group):
            rows = slice(r * group, (r + 1) * group)
            emit(rows, result(rows, post_scale))

    is_last = j == pl.num_programs(1) - 1
    if emit_next:
        @pl.when(is_last)
        def _():
            pre_scale = gpre_ref[...] * (1 + sc_ref[...])

            def emit(rows, x):
                x_out_ref[rows, :] = x
                h_out_ref[rows, :] = (_rms(x) * pre_scale + sh_ref[...]).astype(BF16)

            epilogue(emit)
    else:
        def to_prompt(rows, x):
            yp_ref[rows, :] = x

        def to_sample(rows, x):
            ys_ref[rows, :] = x

        pl.when(is_last & (i < n_p_tiles))(lambda: epilogue(to_prompt))
        pl.when(is_last & (i >= n_p_tiles))(lambda: epilogue(to_sample))


def _ffn(h, w_in, w_out, x, gpost, gate, nxt=None):
    tm, th = 512, 512
    nj = FF_HIDDEN // th
    npt = N_P // tm
    row = lambda i, j: (_cond_row(i, tm), 0, 0)
    vec = pl.BlockSpec((1, D_MODEL), lambda i, j: (0, 0))
    cond = pl.BlockSpec((None, 1, D_MODEL), row)
    tile = pl.BlockSpec((tm, D_MODEL), lambda i, j: (i, 0))
    in_specs = [
        tile,
        pl.BlockSpec((D_MODEL, th), lambda i, j: (0, j)),
        pl.BlockSpec((D_MODEL, th), lambda i, j: (0, nj + j)),
        pl.BlockSpec((th, D_MODEL), lambda i, j: (j, 0)),
        tile, vec, cond,
    ]
    args = [h, w_in, w_in, w_out, x, gpost, gate]
    if nxt is not None:
        in_specs += [vec, cond, cond]
        args += list(nxt)
        out_specs = [tile, tile]
        out_shape = [jax.ShapeDtypeStruct((N_TOK, D_MODEL), F32),
                     jax.ShapeDtypeStruct((N_TOK, D_MODEL), BF16)]
    else:
        out_specs = [
            pl.BlockSpec((tm, D_MODEL), lambda i, j: (jnp.minimum(i, npt - 1), 0)),
            pl.BlockSpec((tm, D_MODEL), lambda i, j: (jnp.maximum(i - npt, 0), 0)),
        ]
        out_shape = [jax.ShapeDtypeStruct((N_P, D_MODEL), F32),
                     jax.ShapeDtypeStruct((N_S, D_MODEL), F32)]
    return pl.pallas_call(
        functools.partial(_ffn_kernel, emit_next=nxt is not None, n_p_tiles=npt),
        grid=(N_TOK // tm, nj),
        in_specs=in_specs,
        out_specs=out_specs,
        out_shape=out_shape,
        scratch_shapes=[pltpu.VMEM((tm, D_MODEL), F32)],
        compiler_params=_cparams(("arbitrary", "arbitrary"), 56),
        name="ffn",
    )(*args)


def kernel(x_prompt, x_sample, cache_k, cache_v, state_C, state_n, state_m, c, c_ctx, w_mod, b_mod, norm_pre1, norm_post1, norm_pre2, norm_post2, w_in, q_norm, k_norm, mlstm_gate_bias, mlstm_norm, conv_w, w_branch, w_out, w_ffn_in, w_ffn_out):
    cond = jnp.concatenate([c_ctx[None], c, jnp.zeros((N_COND - 1 - DEC_BATCH, D_MODEL), F32)], axis=0)
    mod = _modulation(cond, w_mod, b_mod)
    mod = mod.reshape(DEPTH, N_COND, 6, 1, D_MODEL).transpose(0, 2, 1, 3, 4)
    vec = lambda a: a.reshape(1, -1)
    cos, sin = _rope_tables()

    x = (x_prompt.reshape(N_P, D_MODEL), x_sample.reshape(N_S, D_MODEL))
    h = _prenorm(*x, vec(norm_pre1[0]), mod[0, 1], mod[0, 0])

    ck = cache_k.reshape(DEC_BATCH, DEPTH, PAST_LEN, KV_WIDTH)
    cv = cache_v.reshape(DEC_BATCH, DEPTH, PAST_LEN, KV_WIDTH)
    st_n = state_n.reshape(DEC_BATCH, DEPTH, 2, M_HEADS, 1, M_DK)
    w_in_t = jnp.swapaxes(w_in, 1, 2)
    w_branch16 = w_branch.astype(BF16)
    w_out16 = w_out.astype(BF16)

    cache, state = None, None
    for l in range(DEPTH):
        sh1, sc1, g1, sh2, sc2, g2 = (mod[l, i] for i in range(6))
        gate_bias = jnp.pad(mlstm_gate_bias[l], (0, LANES - GATE_COLS)).reshape(1, LANES)
        mnorm_rep = jnp.broadcast_to(mlstm_norm[l][:, None], (M_WIDTH, LANES))

        p32, p16, pt16, w_ffn_in16, w_ffn_out16 = _projection(h, w_in_t, w_ffn_in, w_ffn_out, l)
        ld, rows = _gates(h, w_in_t, l, gate_bias)

        qw, kw = vec(q_norm[l]), vec(k_norm[l])
        att_p, new_k, new_v = _attention_ctx(p32, qw, kw, l, cache)
        cache = (new_k, new_v)
        att_s = _attention_dec(p32, ck, cv, qw, kw, cos, sin, l)

        ht_p, *state = _mlstm(p16, pt16, ld, rows, n_seq=BATCH, seq_len=SEQ, row0=0, layer=l,
                              prev_state=state, emit_state=True)
        m0 = jnp.broadcast_to(state_m[:, l].reshape(DEC_BATCH, 2, M_HEADS, 1, 1),
                              (DEC_BATCH, 2, M_HEADS, 1, LANES))
        (ht_s,) = _mlstm(p16, pt16, ld, rows, n_seq=DEC_BATCH, seq_len=DEC_SEQ, row0=N_P, layer=l,
                         init=(state_C, st_n, m0))

        merged = _merge(att_p, att_s, ht_p, ht_s, p16, pt16, w_branch16, l, mnorm_rep,
                        jnp.pad(conv_w[l], ((0, 5), (0, 0))))
        x, h2 = _outproj(merged, w_out16, l, x, vec(norm_post1[l]), g1,
                         vec(norm_pre2[l]), sc2, sh2)
        if l + 1 < DEPTH:
            nxt = (vec(norm_pre1[l + 1]), mod[l + 1, 1], mod[l + 1, 0])
            x, h = _ffn(h2, w_ffn_in16, w_ffn_out16, x, vec(norm_post2[l]), g2, nxt)
        else:
            y_p, y_s = _ffn(h2, w_ffn_in16, w_ffn_out16, x, vec(norm_post2[l]), g2)

    new_k, new_v = cache
    c_fin, n_fin, m_fin = state
    return (y_p.reshape(BATCH, SEQ, D_MODEL), y_s.reshape(DEC_BATCH, DEC_SEQ, D_MODEL),
            new_k.reshape(BATCH, DEPTH, SEQ, ATT_KV_HEADS, HEAD_DIM),
            new_v.reshape(BATCH, DEPTH, SEQ, ATT_KV_HEADS, HEAD_DIM),
            c_fin, n_fin.reshape(BATCH, DEPTH, 2, M_HEADS, M_DK), m_fin[:, :, :, :, 0, 0])
```

```python
import functools

import jax
import jax.numpy as jnp
from jax import lax
from jax.experimental import pallas as pl
from jax.experimental.pallas import tpu as pltpu

F32 = jnp.float32
BF16 = jnp.bfloat16

D_MODEL = 2048
BATCH = 16
SEQ = 256
DEPTH = 2
DEC_BATCH = 2
DEC_SEQ = 2048
PAST_LEN = 256
GRID_W = 64
EPS = 1e-6
HEAD_DIM = 128
ATT_Q_HEADS = 8
ATT_KV_HEADS = 2
ATT_GROUP = ATT_Q_HEADS // ATT_KV_HEADS
ATT_WIDTH = ATT_Q_HEADS * HEAD_DIM
KV_WIDTH = ATT_KV_HEADS * HEAD_DIM
ROPE_THETA = 10000.0
ROPE_FREQS = HEAD_DIM // 4
M_HEADS = 4
M_DK = 256
M_DV = 256
M_WIDTH = M_HEADS * M_DV
M_CHUNK = 128
CONV_WIDTH = 1024
N_BRANCH = 3
BRANCH_W = 1024
FF_HIDDEN = 5632
IN_WIDTH = 14864

LANES = 128
N_P = BATCH * SEQ
N_S = DEC_BATCH * DEC_SEQ
N_TOK = N_P + N_S
N_CHUNKS = N_TOK // M_CHUNK
N_COND = 8
KV_LEN_S = PAST_LEN + DEC_SEQ

W_AQ, W_AK, W_AV = 0, 1024, 1280
W_MQ, W_MV, W_MG, W_CB = 1536, 3584, 5632, 5648
GATE_COLS = 4 * M_HEADS
PROJ_TN = 512
N_TILE_A = W_MG // PROJ_TN
N_TILE_C = (IN_WIDTH - W_CB) // PROJ_TN
N_TILE_32 = W_MQ // PROJ_TN
TILE_T0 = W_MV // PROJ_TN
TILE_T1 = N_TILE_A
TILE_MO = TILE_T0 + M_WIDTH // PROJ_TN
TILE_GL = N_TILE_A + 3 * CONV_WIDTH // PROJ_TN

P32_W = W_MQ
C16_GL, C16_MQ, C16_MK = 0, 6144, 7168
C16_CB, C16_CC, C16_CX = 8192, 9216, 10240
P16_W = 11264
RT_MV, RT_MO = 0, 1024
PT_H = 2048
M_AUG = M_DV + LANES
GATE_ROWS = 24

NT_DIMS = (((1,), (1,)), ((), ()))
TN_DIMS = (((0,), (0,)), ((), ()))


def _cparams(semantics, vmem_mb):
    return pltpu.CompilerParams(dimension_semantics=semantics,
                                vmem_limit_bytes=vmem_mb * 1024 * 1024)


def _rms(x):
    return x * lax.rsqrt(jnp.mean(x * x, axis=-1, keepdims=True) + EPS)


def _twice_sigmoid_of_twice(x_half):
    return jnp.tanh(x_half) + 1.0


def _sigmoid(x):
    return 0.5 * _twice_sigmoid_of_twice(0.5 * x)


def _cond_row(tile, tm):
    return jnp.where(tile < N_P // tm, 0, 1 + (tile * tm - N_P) // DEC_SEQ)


def _mod_kernel(c_ref, w_ref, b_ref, o_ref):
    c = c_ref[...]
    a = (c * _sigmoid(c)).astype(BF16)
    o_ref[...] = jnp.dot(a, w_ref[...].astype(BF16), preferred_element_type=F32) + b_ref[...]


def _modulation(cond, w_mod, b_mod):
    tn = 1024
    n = 6 * D_MODEL
    return pl.pallas_call(
        _mod_kernel,
        grid=(DEPTH, n // tn),
        in_specs=[
            pl.BlockSpec((N_COND, D_MODEL), lambda l, j: (0, 0)),
            pl.BlockSpec((None, D_MODEL, tn), lambda l, j: (l, 0, j)),
            pl.BlockSpec((None, 1, tn), lambda l, j: (l, 0, j)),
        ],
        out_specs=pl.BlockSpec((None, N_COND, tn), lambda l, j: (l, 0, j)),
        out_shape=jax.ShapeDtypeStruct((DEPTH, N_COND, n), F32),
        compiler_params=_cparams(("parallel", "parallel"), 32),
        name="modulation",
    )(cond, w_mod, b_mod.reshape(DEPTH, 1, n))


def _prenorm_kernel(xp_ref, xs_ref, g_ref, sc_ref, sh_ref, h_ref, *, n_p_tiles):
    def emit(src_ref):
        y = _rms(src_ref[...]) * g_ref[...]
        h_ref[...] = (y * (1 + sc_ref[...]) + sh_ref[...]).astype(BF16)

    i = pl.program_id(0)
    pl.when(i < n_p_tiles)(lambda: emit(xp_ref))
    pl.when(i >= n_p_tiles)(lambda: emit(xs_ref))


def _prenorm(xp, xs, g, sc, sh):
    tm = 512
    npt = N_P // tm
    row = lambda i: (_cond_row(i, tm), 0, 0)
    tile = pl.BlockSpec((tm, D_MODEL), lambda i: (i, 0))
    return pl.pallas_call(
        functools.partial(_prenorm_kernel, n_p_tiles=npt),
        grid=(N_TOK // tm,),
        in_specs=[
            pl.BlockSpec((tm, D_MODEL), lambda i: (jnp.minimum(i, npt - 1), 0)),
            pl.BlockSpec((tm, D_MODEL), lambda i: (jnp.maximum(i - npt, 0), 0)),
            pl.BlockSpec((1, D_MODEL), lambda i: (0, 0)),
            pl.BlockSpec((None, 1, D_MODEL), row),
            pl.BlockSpec((None, 1, D_MODEL), row),
        ],
        out_specs=tile,
        out_shape=jax.ShapeDtypeStruct((N_TOK, D_MODEL), BF16),
        compiler_params=_cparams(("arbitrary",), 32),
        name="prenorm",
    )(xp, xs, g, sc, sh)


def _proj_tile16(j):
    first_gl = N_TILE_A + 3 * CONV_WIDTH // PROJ_TN
    mq0 = C16_MQ // PROJ_TN
    return jnp.where(j < TILE_T0, mq0 + jnp.maximum(j - N_TILE_32, 0),
                     jnp.where(j < TILE_T1, mq0 + TILE_T0 - N_TILE_32 - 1,
                               jnp.where(j < first_gl, C16_CB // PROJ_TN + (j - N_TILE_A),
                                         j - first_gl)))


FIN_ROWS, FOUT_ROWS = 32, 128
N_FIN_BLK = D_MODEL // FIN_ROWS
N_FOUT_BLK = FF_HIDDEN // FOUT_ROWS


def _proj_kernel(x_ref, w_ref, tail_ref, fin_ref, fout_ref,
                 o32_ref, o16_ref, ot_ref, fin16_ref, fout16_ref, w_scr):
    j = pl.program_id(1)
    shifted = j >= N_TILE_A
    off = pl.multiple_of(jnp.where(shifted, GATE_COLS, 0), 8)
    body = PROJ_TN - GATE_COLS
    halved = ((j >= TILE_MO) & (j < TILE_T1)) | (j >= TILE_GL)
    scale = jnp.where(halved, 0.5, 1.0)
    transposed = (j >= TILE_T0) & (j < TILE_T1)

    def prepare():
        w_scr[0:body, :] = (w_ref[pl.ds(off, body), :] * scale).astype(BF16)
        w_scr[body:, :] = (jnp.where(shifted, tail_ref[...], w_ref[body:, :]) * scale).astype(BF16)
        fin16_ref[...] = fin_ref[...].astype(BF16)
        fout16_ref[...] = fout_ref[...].astype(BF16)

    @pl.when(transposed)
    def _():
        prepare()
        acc_t = lax.dot_general(w_scr[...], x_ref[...], NT_DIMS, preferred_element_type=F32)
        ot_ref[...] = acc_t.astype(BF16)

    @pl.when(j < N_TILE_32)
    def _():
        prepare()
        o32_ref[...] = lax.dot_general(x_ref[...], w_scr[...], NT_DIMS, preferred_element_type=F32)

    @pl.when((j >= N_TILE_32) & jnp.logical_not(transposed))
    def _():
        prepare()
        acc = lax.dot_general(x_ref[...], w_scr[...], NT_DIMS, preferred_element_type=F32)
        o16_ref[...] = acc.astype(BF16)


def _projection(h, w_t, w_ffn_in, w_ffn_out, layer):
    tm = 2048
    nj = N_TILE_A + N_TILE_C
    tails_per_tile = PROJ_TN // GATE_COLS
    tail0 = W_MG // GATE_COLS
    assert (N_TOK // tm) * nj >= N_FIN_BLK + N_FOUT_BLK

    def tail_idx(i, j):
        return (layer, jnp.where(j < N_TILE_A, tail0, (j + 1) * tails_per_tile), 0)

    fin_blk = lambda i, j: jnp.minimum(i * nj + j, N_FIN_BLK - 1)
    fout_blk = lambda i, j: jnp.clip(i * nj + j - N_FIN_BLK, 0, N_FOUT_BLK - 1)
    return pl.pallas_call(
        _proj_kernel,
        grid=(N_TOK // tm, nj),
        in_specs=[
            pl.BlockSpec((tm, D_MODEL), lambda i, j: (i, 0), pipeline_mode=pl.Buffered(1)),
            pl.BlockSpec((None, PROJ_TN, D_MODEL), lambda i, j: (layer, j, 0)),
            pl.BlockSpec((None, GATE_COLS, D_MODEL), tail_idx),
            pl.BlockSpec((None, FIN_ROWS, 2 * FF_HIDDEN), lambda i, j: (layer, fin_blk(i, j), 0)),
            pl.BlockSpec((None, FOUT_ROWS, D_MODEL), lambda i, j: (layer, fout_blk(i, j), 0)),
        ],
        out_specs=[
            pl.BlockSpec((tm, PROJ_TN), lambda i, j: (i, jnp.minimum(j, N_TILE_32 - 1))),
            pl.BlockSpec((tm, PROJ_TN), lambda i, j: (i, _proj_tile16(j))),
            pl.BlockSpec((PROJ_TN, tm),
                         lambda i, j: (jnp.clip(j - TILE_T0, 0, TILE_T1 - TILE_T0 - 1), i)),
            pl.BlockSpec((FIN_ROWS, 2 * FF_HIDDEN), lambda i, j: (fin_blk(i, j), 0)),
            pl.BlockSpec((FOUT_ROWS, D_MODEL), lambda i, j: (fout_blk(i, j), 0)),
        ],
        out_shape=[jax.ShapeDtypeStruct((N_TOK, P32_W), F32),
                   jax.ShapeDtypeStruct((N_TOK, P16_W), BF16),
                   jax.ShapeDtypeStruct((PT_H, N_TOK), BF16),
                   jax.ShapeDtypeStruct((D_MODEL, 2 * FF_HIDDEN), BF16),
                   jax.ShapeDtypeStruct((FF_HIDDEN, D_MODEL), BF16)],
        scratch_shapes=[pltpu.VMEM((PROJ_TN, D_MODEL), BF16)],
        compiler_params=_cparams(("arbitrary", "arbitrary"), 56),
        name="projection",
    )(h, w_t, w_t, w_ffn_in, w_ffn_out)


def _gate_kernel(h_ref, w_ref, b_ref, wb_ref, wo_ref, ld_ref, row_ref, wb16_ref, wo16_ref,
                 *, chunks):
    wb16_ref[...] = wb_ref[...].astype(BF16)
    wo16_ref[...] = wo_ref[...].astype(BF16)
    g = lax.dot_general(h_ref[...], w_ref[...].astype(BF16), NT_DIMS,
                        preferred_element_type=F32) + b_ref[...]
    lf = jax.nn.log_sigmoid(g)
    s_idx = lax.broadcasted_iota(jnp.int32, (M_CHUNK, M_CHUNK), 0)
    t_idx = lax.broadcasted_iota(jnp.int32, (M_CHUNK, M_CHUNK), 1)
    tril = (t_idx <= s_idx).astype(BF16)
    lane = lax.broadcasted_iota(jnp.int32, (M_CHUNK, LANES), 1)
    lf_hi = lf.astype(BF16)
    rest = lf - lf_hi.astype(F32)
    lf_mid = rest.astype(BF16)
    lf_lo = (rest - lf_mid.astype(F32)).astype(BF16)
    for ch in range(chunks):
        rows = slice(ch * M_CHUNK, (ch + 1) * M_CHUNK)
        gc = g[rows]
        lfc = lf[rows]
        pre = (jnp.dot(tril, lf_hi[rows], preferred_element_type=F32)
               + jnp.dot(tril, lf_mid[rows], preferred_element_type=F32)
               + jnp.dot(tril, lf_lo[rows], preferred_element_type=F32))
        tot = jnp.broadcast_to(pre[M_CHUNK - 1:M_CHUNK, :], pre.shape)
        suf = tot - pre + lfc
        both = jnp.where((lane >= 4) & (lane < 8), pre,
                         jnp.where((lane >= 12) & (lane < 16), suf, gc))
        both_t = both.T
        tot_t = tot.T
        for d in range(2):
            mask = (s_idx <= t_idx) if d == 0 else (s_idx >= t_idx)
            row_ref[d, ch, 20:GATE_ROWS, :] = jnp.zeros((GATE_ROWS - 20, M_CHUNK), F32)
            for hd in range(M_HEADS):
                li, lb = 8 * d + hd, 8 * d + 4 + hd
                i_row = both_t[li:li + 1, :]
                b_row = both_t[lb:lb + 1, :]
                bl_row = tot_t[lb:lb + 1, :]
                key = both[:, lb:lb + 1] - both[:, li:li + 1]
                ld = jnp.where(mask, b_row - key, -jnp.inf)
                wl = bl_row - b_row + i_row
                ld_ref[d, hd, rows, :] = ld
                row_ref[d, ch, hd:hd + 1, :] = i_row
                row_ref[d, ch, 4 + hd:5 + hd, :] = b_row
                row_ref[d, ch, 8 + hd:9 + hd, :] = jnp.max(ld, axis=0, keepdims=True)
                row_ref[d, ch, 12 + hd:13 + hd, :] = bl_row
                row_ref[d, ch, 16 + hd:17 + hd, :] = jnp.broadcast_to(
                    jnp.max(wl, axis=-1, keepdims=True), (1, M_CHUNK))


def _gates(h, w_t, layer, bias, w_branch, w_out):
    tm = 512
    chunks = tm // M_CHUNK
    steps = N_TOK // tm
    wb_rows = N_BRANCH * BRANCH_W // steps
    wo_rows = D_MODEL // steps
    return pl.pallas_call(
        functools.partial(_gate_kernel, chunks=chunks),
        grid=(steps,),
        in_specs=[
            pl.BlockSpec((tm, D_MODEL), lambda i: (i, 0)),
            pl.BlockSpec((None, LANES, D_MODEL), lambda i: (layer, W_MG // LANES, 0)),
            pl.BlockSpec((1, LANES), lambda i: (0, 0)),
            pl.BlockSpec((None, wb_rows, D_MODEL), lambda i: (layer, i, 0)),
            pl.BlockSpec((None, wo_rows, D_MODEL), lambda i: (layer, i, 0)),
        ],
        out_specs=[
            pl.BlockSpec((2, M_HEADS, tm, M_CHUNK), lambda i: (0, 0, i, 0)),
            pl.BlockSpec((2, chunks, GATE_ROWS, M_CHUNK), lambda i: (0, i, 0, 0)),
            pl.BlockSpec((wb_rows, D_MODEL), lambda i: (i, 0)),
            pl.BlockSpec((wo_rows, D_MODEL), lambda i: (i, 0)),
        ],
        out_shape=[
            jax.ShapeDtypeStruct((2, M_HEADS, N_TOK, M_CHUNK), F32),
            jax.ShapeDtypeStruct((2, N_CHUNKS, GATE_ROWS, M_CHUNK), F32),
            jax.ShapeDtypeStruct((N_BRANCH * BRANCH_W, D_MODEL), BF16),
            jax.ShapeDtypeStruct((D_MODEL, D_MODEL), BF16),
        ],
        compiler_params=_cparams(("arbitrary",), 32),
        name="mlstm_gates",
    )(h, w_t, bias, w_branch.reshape(DEPTH, N_BRANCH * BRANCH_W, D_MODEL), w_out)


def _mlstm_kernel(*refs, n_par, has_init, emit_state, has_prev):
    pos = 0
    seq_refs = []
    for _ in range(n_par):
        seq_refs.append(refs[pos:pos + 5])
        pos += 5
    if has_init:
        c0_ref, n0_ref, m0_ref = refs[pos:pos + 3]
        pos += 3
    if has_prev:
        pos += 3
    h_ref = refs[pos]
    pos += 1
    if emit_state:
        cout_ref, nout_ref, mout_ref = refs[pos:pos + 3]
        pos += 3
    c_scr, m_scr = refs[pos:pos + 2]

    ci = pl.program_id(2)
    last = pl.num_programs(2) - 1
    pad_rows = lax.broadcasted_iota(jnp.int32, (LANES, M_CHUNK), 0)
    one_row = (pad_rows == 0).astype(F32)

    @pl.when(ci == 0)
    def _():
        if has_init:
            for u in range(n_par):
                for hd in range(M_HEADS):
                    c_scr[u, hd, 0:M_DV, :] = c0_ref[u, hd].T
                    pad = lax.broadcasted_iota(jnp.int32, (LANES, M_DK), 0)
                    c_scr[u, hd, M_DV:M_AUG, :] = jnp.where(pad == 0, n0_ref[u, hd], 0.0)
            m_scr[...] = m0_ref[...]
        else:
            c_scr[...] = jnp.zeros_like(c_scr)
            m_scr[...] = jnp.zeros_like(m_scr)

    for u in range(n_par):
        q_ref, k_ref, vt_ref, ld_ref, row_ref = seq_refs[u]
        for hd in range(M_HEADS):
            cols = slice(hd * M_DK, (hd + 1) * M_DK)
            q = q_ref[:, cols]
            k = k_ref[:, cols] * (M_DK ** -0.5)
            vt_aug = jnp.concatenate([vt_ref[cols, :].astype(F32), one_row], axis=0)
            ld = ld_ref[hd]
            i_row = row_ref[hd:hd + 1, :]
            b_row = row_ref[4 + hd:5 + hd, :]
            ldmax_row = row_ref[8 + hd:9 + hd, :]
            bl_row = row_ref[12 + hd:13 + hd, :]
            wmax_row = row_ref[16 + hd:17 + hd, :]
            mem = c_scr[u, hd]
            m_prev = m_scr[u, hd]

            g_row = b_row + m_prev
            mt_row = jnp.maximum(g_row, ldmax_row)
            kq = lax.dot_general(k, q, NT_DIMS, preferred_element_type=F32)
            st = (kq * jnp.exp(ld - mt_row)).astype(BF16)
            nd = jnp.exp(g_row - mt_row) * lax.dot_general(
                mem.astype(BF16), q, NT_DIMS, preferred_element_type=F32) \
                + jnp.dot(vt_aug.astype(BF16), st, preferred_element_type=F32)
            den_row = nd[M_DV:M_DV + 1, :]
            scale_row = 1.0 / jnp.maximum(jnp.abs(den_row), jnp.exp(-mt_row))
            h_ref[u, cols, :] = nd[0:M_DV, :] * scale_row

            m_new = jnp.maximum(bl_row + m_prev, wmax_row)
            w_row = jnp.exp(bl_row - b_row + i_row - m_new)
            dec = jnp.exp(bl_row + m_prev - m_new)
            dec_wide = jnp.concatenate([dec] * (M_DK // LANES), axis=1)
            c_scr[u, hd] = dec_wide * mem + jnp.dot((vt_aug * w_row).astype(BF16), k,
                                                    preferred_element_type=F32)
            m_scr[u, hd] = m_new

    if emit_state:
        @pl.when(ci == last)
        def _():
            if has_prev:
                slot = lambda ref: ref
            else:
                slot = lambda ref: ref.at[:, 0]
                for ref in (cout_ref, nout_ref, mout_ref):
                    for l in range(1, DEPTH):
                        ref[:, l] = jnp.zeros(ref.shape[:1] + ref.shape[2:], F32)
            for u in range(n_par):
                for hd in range(M_HEADS):
                    slot(cout_ref)[u, hd] = c_scr[u, hd, 0:M_DV, :].T
                    slot(nout_ref)[u, hd] = c_scr[u, hd, M_DV:M_DV + 1, :]
            slot(mout_ref)[...] = m_scr[...]


def _mlstm(p16, pt16, ld, rows, *, n_seq, seq_len, row0, layer, init=None, prev_state=None,
           emit_state=False):
    n_par = 2
    nc = seq_len // M_CHUNK
    blk0 = row0 // M_CHUNK

    def chunk_of(c, d):
        return jnp.where(d == 0, c, nc - 1 - c)

    in_specs, args = [], []
    for u in range(n_par):
        blk = lambda g, d, c, u=u: blk0 + (g * n_par + u) * nc + chunk_of(c, d)
        in_specs += [
            pl.BlockSpec((M_CHUNK, M_WIDTH), lambda g, d, c, blk=blk: (blk(g, d, c), C16_MQ // M_WIDTH)),
            pl.BlockSpec((M_CHUNK, M_WIDTH), lambda g, d, c, blk=blk: (blk(g, d, c), C16_MK // M_WIDTH)),
            pl.BlockSpec((M_WIDTH, M_CHUNK), lambda g, d, c, blk=blk: (RT_MV // M_WIDTH, blk(g, d, c))),
            pl.BlockSpec((None, M_HEADS, M_CHUNK, M_CHUNK), lambda g, d, c, blk=blk: (d, 0, blk(g, d, c), 0)),
            pl.BlockSpec((None, None, GATE_ROWS, M_CHUNK), lambda g, d, c, blk=blk: (d, blk(g, d, c), 0, 0)),
        ]
        args += [p16, p16, pt16, ld, rows]
    if init is not None:
        in_specs += [
            pl.BlockSpec((n_par, None, None, M_HEADS, M_DK, M_DV), lambda g, d, c: (g, layer, d, 0, 0, 0)),
            pl.BlockSpec((n_par, None, None, M_HEADS, 1, M_DK), lambda g, d, c: (g, layer, d, 0, 0, 0)),
            pl.BlockSpec((n_par, None, M_HEADS, 1, LANES), lambda g, d, c: (g, d, 0, 0, 0)),
        ]
        args += list(init)
    aliases = {}
    if prev_state is not None:
        for k_, a in enumerate(prev_state):
            aliases[len(args)] = 1 + k_
            in_specs.append(pl.BlockSpec(memory_space=pl.ANY))
            args.append(a)
    out_specs = [pl.BlockSpec((None, n_par, None, M_WIDTH, M_CHUNK),
                              lambda g, d, c: (d, g, chunk_of(c, d), 0, 0))]
    out_shape = [jax.ShapeDtypeStruct((2, n_seq, nc, M_WIDTH, M_CHUNK), F32)]
    if emit_state:
        if prev_state is None:
            lblk, lidx = DEPTH, 0
        else:
            lblk, lidx = None, layer
        out_specs += [
            pl.BlockSpec((n_par, lblk, None, M_HEADS, M_DK, M_DV), lambda g, d, c: (g, lidx, d, 0, 0, 0)),
            pl.BlockSpec((n_par, lblk, None, M_HEADS, 1, M_DK), lambda g, d, c: (g, lidx, d, 0, 0, 0)),
            pl.BlockSpec((n_par, lblk, None, M_HEADS, 1, LANES), lambda g, d, c: (g, lidx, d, 0, 0, 0)),
        ]
        out_shape += [
            jax.ShapeDtypeStruct((n_seq, DEPTH, 2, M_HEADS, M_DK, M_DV), F32),
            jax.ShapeDtypeStruct((n_seq, DEPTH, 2, M_HEADS, 1, M_DK), F32),
            jax.ShapeDtypeStruct((n_seq, DEPTH, 2, M_HEADS, 1, LANES), F32),
        ]
    return pl.pallas_call(
        functools.partial(_mlstm_kernel, n_par=n_par, has_init=init is not None,
                          emit_state=emit_state, has_prev=prev_state is not None),
        grid=(n_seq // n_par, 2, nc),
        in_specs=in_specs,
        out_specs=out_specs,
        out_shape=out_shape,
        input_output_aliases=aliases,
        scratch_shapes=[
            pltpu.VMEM((n_par, M_HEADS, M_AUG, M_DK), F32),
            pltpu.VMEM((n_par, M_HEADS, 1, LANES), F32),
        ],
        compiler_params=_cparams(("arbitrary", "arbitrary", "arbitrary"), 40),
        name="mlstm_scan_dec" if init is not None else "mlstm_scan_ctx",
    )(*args)


def _swap32(x):
    lane = lax.broadcasted_iota(jnp.int32, x.shape, 1)
    return jnp.where((lane % 64) < 32, pltpu.roll(x, HEAD_DIM - 32, axis=1),
                     pltpu.roll(x, 32, axis=1))


def _group_attention(q_heads, kb, vb, o_ref):
    for g, q in enumerate(q_heads):
        s = lax.dot_general(q.astype(BF16), kb, NT_DIMS, preferred_element_type=F32)
        e = jnp.exp2(s - jnp.max(s, axis=-1, keepdims=True))
        l = jnp.sum(e, axis=-1, keepdims=True)
        o = jnp.dot(e.astype(BF16), vb, preferred_element_type=F32) / l
        o_ref[:, g * HEAD_DIM:(g + 1) * HEAD_DIM] = o.astype(BF16)


Q_SCALE = HEAD_DIM ** -0.5 * 1.4426950408889634


def _attn_ctx_kernel(*refs, first_layer):
    q_ref, k_ref, v_ref, qw_ref, kw_ref = refs[:5]
    o_ref, kout_ref, vout_ref = refs[-3:]
    k = _rms(k_ref[...]) * kw_ref[...]
    v = v_ref[...]
    if first_layer:
        kout_ref[0] = k
        vout_ref[0] = v
        for l in range(1, DEPTH):
            kout_ref[l] = jnp.zeros_like(k)
            vout_ref[l] = jnp.zeros_like(v)
    else:
        kout_ref[...] = k
        vout_ref[...] = v
    q_heads = [_rms(q_ref[:, g * HEAD_DIM:(g + 1) * HEAD_DIM]) * qw_ref[...] * Q_SCALE
               for g in range(ATT_GROUP)]
    _group_attention(q_heads, k.astype(BF16), v.astype(BF16), o_ref)


def _attention_ctx(p32, qw, kw, layer, prev_cache=None):
    gw = ATT_GROUP * HEAD_DIM
    in_specs = [
        pl.BlockSpec((SEQ, gw), lambda b, h: (b, W_AQ // gw + h)),
        pl.BlockSpec((SEQ, HEAD_DIM), lambda b, h: (b, W_AK // HEAD_DIM + h)),
        pl.BlockSpec((SEQ, HEAD_DIM), lambda b, h: (b, W_AV // HEAD_DIM + h)),
        pl.BlockSpec((1, HEAD_DIM), lambda b, h: (0, 0)),
        pl.BlockSpec((1, HEAD_DIM), lambda b, h: (0, 0)),
    ]
    args = [p32, p32, p32, qw, kw]
    aliases = {}
    if prev_cache is not None:
        for k_, a in enumerate(prev_cache):
            aliases[len(args)] = 1 + k_
            in_specs.append(pl.BlockSpec(memory_space=pl.ANY))
            args.append(a)
    if prev_cache is None:
        cache_spec = pl.BlockSpec((None, DEPTH, SEQ, HEAD_DIM), lambda b, h: (b, 0, 0, h))
    else:
        cache_spec = pl.BlockSpec((None, None, SEQ, HEAD_DIM), lambda b, h: (b, layer, 0, h))
    cache_shape = jax.ShapeDtypeStruct((BATCH, DEPTH, SEQ, KV_WIDTH), F32)
    return pl.pallas_call(
        functools.partial(_attn_ctx_kernel, first_layer=prev_cache is None),
        grid=(BATCH, ATT_KV_HEADS),
        in_specs=in_specs,
        out_specs=[pl.BlockSpec((SEQ, gw), lambda b, h: (b, h)), cache_spec, cache_spec],
        out_shape=[jax.ShapeDtypeStruct((N_P, ATT_WIDTH), BF16), cache_shape, cache_shape],
        input_output_aliases=aliases,
        compiler_params=_cparams(("arbitrary", "arbitrary"), 32),
        name="attention_ctx",
    )(*args)


def _attn_dec_kernel(q_ref, k_ref, v_ref, ck_ref, cv_ref, qw_ref, kw_ref,
                     cos_ref, sin_ref, cosq_ref, sinq_ref, o_ref, k_scr, v_scr):
    @pl.when(pl.program_id(2) == 0)
    def _():
        k_scr[0:PAST_LEN, :] = ck_ref[...].astype(BF16)
        v_scr[0:PAST_LEN, :] = cv_ref[...].astype(BF16)
        k = _rms(k_ref[...]) * kw_ref[...]
        k = k * cos_ref[...] + _swap32(k) * sin_ref[...]
        k_scr[PAST_LEN:, :] = k.astype(BF16)
        v_scr[PAST_LEN:, :] = v_ref[...].astype(BF16)

    q_heads = []
    for g in range(ATT_GROUP):
        q = _rms(q_ref[:, g * HEAD_DIM:(g + 1) * HEAD_DIM]) * qw_ref[...]
        q_heads.append((q * cosq_ref[...] + _swap32(q) * sinq_ref[...]) * Q_SCALE)
    _group_attention(q_heads, k_scr[...], v_scr[...], o_ref)


def _attention_dec(p32, ck, cv, qw, kw, cos, sin, layer):
    tq = 256
    gw = ATT_GROUP * HEAD_DIM
    nq = DEC_SEQ // tq
    seq0 = N_P // DEC_SEQ
    cache_spec = pl.BlockSpec((None, None, PAST_LEN, HEAD_DIM), lambda b, h, i: (b, layer, 0, h))
    return pl.pallas_call(
        _attn_dec_kernel,
        grid=(DEC_BATCH, ATT_KV_HEADS, nq),
        in_specs=[
            pl.BlockSpec((tq, gw), lambda b, h, i: (N_P // tq + b * nq + i, W_AQ // gw + h)),
            pl.BlockSpec((DEC_SEQ, HEAD_DIM), lambda b, h, i: (seq0 + b, W_AK // HEAD_DIM + h)),
            pl.BlockSpec((DEC_SEQ, HEAD_DIM), lambda b, h, i: (seq0 + b, W_AV // HEAD_DIM + h)),
            cache_spec, cache_spec,
            pl.BlockSpec((1, HEAD_DIM), lambda b, h, i: (0, 0)),
            pl.BlockSpec((1, HEAD_DIM), lambda b, h, i: (0, 0)),
            pl.BlockSpec((DEC_SEQ, HEAD_DIM), lambda b, h, i: (0, 0)),
            pl.BlockSpec((DEC_SEQ, HEAD_DIM), lambda b, h, i: (0, 0)),
            pl.BlockSpec((tq, HEAD_DIM), lambda b, h, i: (i, 0)),
            pl.BlockSpec((tq, HEAD_DIM), lambda b, h, i: (i, 0)),
        ],
        out_specs=pl.BlockSpec((tq, gw), lambda b, h, i: (b * nq + i, h)),
        out_shape=jax.ShapeDtypeStruct((N_S, ATT_WIDTH), BF16),
        scratch_shapes=[
            pltpu.VMEM((KV_LEN_S, HEAD_DIM), BF16),
            pltpu.VMEM((KV_LEN_S, HEAD_DIM), BF16),
        ],
        compiler_params=_cparams(("arbitrary", "arbitrary", "arbitrary"), 32),
        name="attention_dec",
    )(p32, p32, p32, ck, cv, qw, kw, cos, sin, cos, sin)


def _rope_tables():
    t = jnp.arange(DEC_SEQ)
    inv = ROPE_THETA ** (-jnp.arange(ROPE_FREQS, dtype=F32) / ROPE_FREQS)
    ang = jnp.stack([t // GRID_W, t % GRID_W], axis=-1).astype(F32)[:, :, None] * inv
    cos = jnp.cos(ang)
    sin = jnp.sin(ang)
    cos = jnp.stack([cos, cos], axis=2).reshape(DEC_SEQ, HEAD_DIM)
    sin = jnp.stack([-sin, sin], axis=2).reshape(DEC_SEQ, HEAD_DIM)
    return cos, sin


def _merge_kernel(attp_ref, atts_ref, hfp_ref, hbp_ref, hfs_ref, hbs_ref,
                  mot_ref, cb_ref, cc_ref, cx_ref, ccp_ref, cxp_ref, ccn_ref, cxn_ref,
                  gl0_ref, gl1_ref, gl2_ref, wb_ref, mn_ref, cw_ref, o_ref, att_scr, hsum_scr,
                  *, tm, halo):
    i = pl.program_id(0)
    row0 = i * tm
    in_p = row0 < N_P
    off = jnp.where(in_p, row0 % SEQ, (row0 - N_P) % DEC_SEQ)
    seq_len = jnp.where(in_p, SEQ, DEC_SEQ)
    first = off == 0
    final = off + tm == seq_len

    @pl.when(in_p)
    def _():
        att_scr[...] = attp_ref[...]
        hsum_scr[...] = hfp_ref[...] + hbp_ref[...]

    @pl.when(jnp.logical_not(in_p))
    def _():
        att_scr[...] = atts_ref[...]
        hsum_scr[...] = hfs_ref[...] + hbs_ref[...]

    mn_half = 0.5 * mn_ref[...]
    chunk_parts = []
    for ch in range(tm // M_CHUNK):
        parts = []
        for hd in range(M_HEADS):
            x = hsum_scr[ch, hd * M_DV:(hd + 1) * M_DV, :]
            parts.append(x * lax.rsqrt(jnp.mean(x * x, axis=0, keepdims=True) + EPS))
        chunk_parts.append(jnp.concatenate(parts, axis=0) * mn_half)
    ml_t = jnp.concatenate(chunk_parts, axis=1) * _twice_sigmoid_of_twice(
        mot_ref[...].astype(F32))

    u = cc_ref[...].astype(F32) * cx_ref[...].astype(F32)
    u_halo_prev = ccp_ref[...].astype(F32) * cxp_ref[...].astype(F32)
    u_halo_next = ccn_ref[...].astype(F32) * cxn_ref[...].astype(F32)
    u_prev_row = jnp.where(first, 0.0, u_halo_prev[halo - 1:halo, :])
    u_next_row = jnp.where(final, 0.0, u_halo_next[0:1, :])
    ridx = lax.broadcasted_iota(jnp.int32, u.shape, 0)
    u_prev = jnp.where(ridx == 0, u_prev_row, pltpu.roll(u, 1, axis=0))
    u_next = jnp.where(ridx == tm - 1, u_next_row, pltpu.roll(u, tm - 1, axis=0))
    cw = cw_ref[...]
    cv = cb_ref[...].astype(F32) * (cw[0:1, :] * u_prev + cw[1:2, :] * u + cw[2:3, :] * u_next)

    acc = _twice_sigmoid_of_twice(gl0_ref[...].astype(F32)) * jnp.dot(
        att_scr[...], wb_ref[0], preferred_element_type=F32)
    acc += _twice_sigmoid_of_twice(gl1_ref[...].astype(F32)) * lax.dot_general(
        ml_t.astype(BF16), wb_ref[1], TN_DIMS, preferred_element_type=F32)
    acc += _twice_sigmoid_of_twice(gl2_ref[...].astype(F32)) * jnp.dot(
        cv.astype(BF16), wb_ref[2], preferred_element_type=F32)
    o_ref[...] = (0.5 * acc).astype(BF16)


def _merge(att_p, att_s, ht_p, ht_s, p16, pt16, wb, mnorm_rep, convw):
    tm = 2 * M_CHUNK
    cpt = tm // M_CHUNK
    halo = 16
    nt = N_TOK // tm
    npt = N_P // tm
    th = tm // halo
    last_h = N_TOK // halo - 1
    p_idx = lambda i: jnp.minimum(i, npt - 1)
    s_idx = lambda i: jnp.maximum(i - npt, 0)
    cblk = lambda col: (lambda i: (i, col // BRANCH_W))
    prevh = lambda col: (lambda i: (jnp.maximum(i * th - 1, 0), col // BRANCH_W))
    nexth = lambda col: (lambda i: (jnp.minimum((i + 1) * th, last_h), col // BRANCH_W))
    glblk = lambda g: (lambda i: (i, C16_GL // D_MODEL + g))
    ht_p = ht_p.reshape(2, N_P // M_CHUNK, M_WIDTH, M_CHUNK)
    ht_s = ht_s.reshape(2, N_S // M_CHUNK, M_WIDTH, M_CHUNK)
    ht_spec = lambda d, idx: pl.BlockSpec((None, cpt, M_WIDTH, M_CHUNK), lambda i: (d, idx(i), 0, 0))
    return pl.pallas_call(
        functools.partial(_merge_kernel, tm=tm, halo=halo),
        grid=(nt,),
        in_specs=[
            pl.BlockSpec((tm, BRANCH_W), lambda i: (p_idx(i), 0)),
            pl.BlockSpec((tm, BRANCH_W), lambda i: (s_idx(i), 0)),
            ht_spec(0, p_idx), ht_spec(1, p_idx), ht_spec(0, s_idx), ht_spec(1, s_idx),
            pl.BlockSpec((M_WIDTH, tm), lambda i: (RT_MO // M_WIDTH, i)),
            pl.BlockSpec((tm, BRANCH_W), cblk(C16_CB)),
            pl.BlockSpec((tm, BRANCH_W), cblk(C16_CC)),
            pl.BlockSpec((tm, BRANCH_W), cblk(C16_CX)),
            pl.BlockSpec((halo, BRANCH_W), prevh(C16_CC)),
            pl.BlockSpec((halo, BRANCH_W), prevh(C16_CX)),
            pl.BlockSpec((halo, BRANCH_W), nexth(C16_CC)),
            pl.BlockSpec((halo, BRANCH_W), nexth(C16_CX)),
            pl.BlockSpec((tm, D_MODEL), glblk(0)),
            pl.BlockSpec((tm, D_MODEL), glblk(1)),
            pl.BlockSpec((tm, D_MODEL), glblk(2)),
            pl.BlockSpec((N_BRANCH, BRANCH_W, D_MODEL), lambda i: (0, 0, 0),
                         pipeline_mode=pl.Buffered(1)),
            pl.BlockSpec((M_WIDTH, LANES), lambda i: (0, 0)),
            pl.BlockSpec((8, CONV_WIDTH), lambda i: (0, 0)),
        ],
        out_specs=pl.BlockSpec((tm, D_MODEL), lambda i: (i, 0)),
        out_shape=jax.ShapeDtypeStruct((N_TOK, D_MODEL), BF16),
        scratch_shapes=[pltpu.VMEM((tm, BRANCH_W), BF16),
                        pltpu.VMEM((cpt, M_WIDTH, M_CHUNK), F32)],
        compiler_params=_cparams(("arbitrary",), 52),
        name="branch_merge",
    )(att_p, att_s, ht_p, ht_p, ht_s, ht_s, pt16, p16, p16, p16, p16, p16, p16, p16,
      p16, p16, p16, wb, mnorm_rep, convw)


def _outproj_kernel(*refs, split_x, n_p_tiles):
    m_ref, w_ref = refs[:2]
    if split_x:
        xp_ref, xs_ref = refs[2:4]
        rest = refs[4:]
    else:
        x_ref = refs[2]
        rest = refs[3:]
    gpost_ref, gate_ref, gpre_ref, sc_ref, sh_ref, x_out_ref, h_out_ref = rest
    in_p = pl.program_id(0) < n_p_tiles
    post_scale = gate_ref[...] * gpost_ref[...]
    pre_scale = gpre_ref[...] * (1 + sc_ref[...])
    group = 128
    for r in range(m_ref.shape[0] // group):
        rows = slice(r * group, (r + 1) * group)
        if split_x:
            x_in = jnp.where(in_p, xp_ref[rows, :], xs_ref[rows, :])
        else:
            x_in = x_ref[rows, :]
        mix = jnp.dot(m_ref[rows, :], w_ref[...], preferred_element_type=F32)
        x = x_in + _rms(mix) * post_scale
        x_out_ref[rows, :] = x
        h_out_ref[rows, :] = (_rms(x) * pre_scale + sh_ref[...]).astype(BF16)


def _outproj(merged, w_out, x, gpost, gate, gpre, sc, sh):
    tm = 512
    npt = N_P // tm
    row = lambda i: (_cond_row(i, tm), 0, 0)
    vec = pl.BlockSpec((1, D_MODEL), lambda i: (0, 0))
    cond = pl.BlockSpec((None, 1, D_MODEL), row)
    tile = pl.BlockSpec((tm, D_MODEL), lambda i: (i, 0))
    split_x = isinstance(x, tuple)
    if split_x:
        x_specs = [pl.BlockSpec((tm, D_MODEL), lambda i: (jnp.minimum(i, npt - 1), 0)),
                   pl.BlockSpec((tm, D_MODEL), lambda i: (jnp.maximum(i - npt, 0), 0))]
        x_args = list(x)
    else:
        x_specs, x_args = [tile], [x]
    return pl.pallas_call(
        functools.partial(_outproj_kernel, split_x=split_x, n_p_tiles=npt),
        grid=(N_TOK // tm,),
        in_specs=[tile,
                  pl.BlockSpec((D_MODEL, D_MODEL), lambda i: (0, 0), pipeline_mode=pl.Buffered(1)),
                  *x_specs, vec, cond, vec, cond, cond],
        out_specs=[tile, tile],
        out_shape=[jax.ShapeDtypeStruct((N_TOK, D_MODEL), F32),
                   jax.ShapeDtypeStruct((N_TOK, D_MODEL), BF16)],
        compiler_params=_cparams(("arbitrary",), 48),
        name="out_projection",
    )(merged, w_out, *x_args, gpost, gate, gpre, sc, sh)


def _ffn_kernel(*refs, emit_next, n_p_tiles):
    h_ref, wg_ref, wu_ref, wo_ref, x_ref, gpost_ref, gate_ref = refs[:7]
    if emit_next:
        gpre_ref, sc_ref, sh_ref, x_out_ref, h_out_ref, acc_ref = refs[7:]
    else:
        yp_ref, ys_ref, acc_ref = refs[7:]

    i = pl.program_id(0)
    j = pl.program_id(1)

    @pl.when(j == 0)
    def _():
        acc_ref[...] = jnp.zeros_like(acc_ref)

    h = h_ref[...]
    gate = jnp.dot(h, wg_ref[...], preferred_element_type=F32)
    up = jnp.dot(h, wu_ref[...], preferred_element_type=F32)
    gate_half = 0.5 * gate
    act = (gate_half * _twice_sigmoid_of_twice(gate_half) * up).astype(BF16)
    acc_ref[...] += jnp.dot(act, wo_ref[...], preferred_element_type=F32)

    group = 64

    def result(rows, post_scale):
        return x_ref[rows, :] + _rms(acc_ref[rows, :]) * post_scale

    def epilogue(emit):
        post_scale = gate_ref[...] * gpost_ref[...]
        for r in range(acc_ref.shape[0] // group):
            rows = slice(r * group, (r + 1) * group)
            emit(rows, result(rows, post_scale))

    is_last = j == pl.num_programs(1) - 1
    if emit_next:
        @pl.when(is_last)
        def _():
            pre_scale = gpre_ref[...] * (1 + sc_ref[...])

            def emit(rows, x):
                x_out_ref[rows, :] = x
                h_out_ref[rows, :] = (_rms(x) * pre_scale + sh_ref[...]).astype(BF16)

            epilogue(emit)
    else:
        def to_prompt(rows, x):
            yp_ref[rows, :] = x

        def to_sample(rows, x):
            ys_ref[rows, :] = x

        pl.when(is_last & (i < n_p_tiles))(lambda: epilogue(to_prompt))
        pl.when(is_last & (i >= n_p_tiles))(lambda: epilogue(to_sample))


def _ffn(h, w_in, w_out, x, gpost, gate, nxt=None):
    tm, th = 512, 512
    nj = FF_HIDDEN // th
    npt = N_P // tm
    row = lambda i, j: (_cond_row(i, tm), 0, 0)
    vec = pl.BlockSpec((1, D_MODEL), lambda i, j: (0, 0))
    cond = pl.BlockSpec((None, 1, D_MODEL), row)
    tile = pl.BlockSpec((tm, D_MODEL), lambda i, j: (i, 0))
    in_specs = [
        tile,
        pl.BlockSpec((D_MODEL, th), lambda i, j: (0, j)),
        pl.BlockSpec((D_MODEL, th), lambda i, j: (0, nj + j)),
        pl.BlockSpec((th, D_MODEL), lambda i, j: (j, 0)),
        tile, vec, cond,
    ]
    args = [h, w_in, w_in, w_out, x, gpost, gate]
    if nxt is not None:
        in_specs += [vec, cond, cond]
        args += list(nxt)
        out_specs = [tile, tile]
        out_shape = [jax.ShapeDtypeStruct((N_TOK, D_MODEL), F32),
                     jax.ShapeDtypeStruct((N_TOK, D_MODEL), BF16)]
    else:
        out_specs = [
            pl.BlockSpec((tm, D_MODEL), lambda i, j: (jnp.minimum(i, npt - 1), 0)),
            pl.BlockSpec((tm, D_MODEL), lambda i, j: (jnp.maximum(i - npt, 0), 0)),
        ]
        out_shape = [jax.ShapeDtypeStruct((N_P, D_MODEL), F32),
                     jax.ShapeDtypeStruct((N_S, D_MODEL), F32)]
    return pl.pallas_call(
        functools.partial(_ffn_kernel, emit_next=nxt is not None, n_p_tiles=npt),
        grid=(N_TOK // tm, nj),
        in_specs=in_specs,
        out_specs=out_specs,
        out_shape=out_shape,
        scratch_shapes=[pltpu.VMEM((tm, D_MODEL), F32)],
        compiler_params=_cparams(("arbitrary", "arbitrary"), 56),
        name="ffn",
    )(*args)


def kernel(x_prompt, x_sample, cache_k, cache_v, state_C, state_n, state_m, c, c_ctx, w_mod, b_mod, norm_pre1, norm_post1, norm_pre2, norm_post2, w_in, q_norm, k_norm, mlstm_gate_bias, mlstm_norm, conv_w, w_branch, w_out, w_ffn_in, w_ffn_out):
    cond = jnp.concatenate([c_ctx[None], c, jnp.zeros((N_COND - 1 - DEC_BATCH, D_MODEL), F32)], axis=0)
    mod = _modulation(cond, w_mod, b_mod)
    mod = mod.reshape(DEPTH, N_COND, 6, 1, D_MODEL).transpose(0, 2, 1, 3, 4)
    vec = lambda a: a.reshape(1, -1)
    cos, sin = _rope_tables()

    x = (x_prompt.reshape(N_P, D_MODEL), x_sample.reshape(N_S, D_MODEL))
    h = _prenorm(*x, vec(norm_pre1[0]), mod[0, 1], mod[0, 0])

    ck = cache_k.reshape(DEC_BATCH, DEPTH, PAST_LEN, KV_WIDTH)
    cv = cache_v.reshape(DEC_BATCH, DEPTH, PAST_LEN, KV_WIDTH)
    st_n = state_n.reshape(DEC_BATCH, DEPTH, 2, M_HEADS, 1, M_DK)
    w_in_t = jnp.swapaxes(w_in, 1, 2)

    cache, state = None, None
    for l in range(DEPTH):
        sh1, sc1, g1, sh2, sc2, g2 = (mod[l, i] for i in range(6))
        gate_bias = jnp.pad(mlstm_gate_bias[l], (0, LANES - GATE_COLS)).reshape(1, LANES)
        mnorm_rep = jnp.broadcast_to(mlstm_norm[l][:, None], (M_WIDTH, LANES))

        p32, p16, pt16, w_ffn_in16, w_ffn_out16 = _projection(h, w_in_t, w_ffn_in, w_ffn_out, l)
        ld, rows, w_branch16, w_out16 = _gates(h, w_in_t, l, gate_bias, w_branch, w_out)

        qw, kw = vec(q_norm[l]), vec(k_norm[l])
        att_p, new_k, new_v = _attention_ctx(p32, qw, kw, l, cache)
        cache = (new_k, new_v)
        att_s = _attention_dec(p32, ck, cv, qw, kw, cos, sin, l)

        ht_p, *state = _mlstm(p16, pt16, ld, rows, n_seq=BATCH, seq_len=SEQ, row0=0, layer=l,
                              prev_state=state, emit_state=True)
        m0 = jnp.broadcast_to(state_m[:, l].reshape(DEC_BATCH, 2, M_HEADS, 1, 1),
                              (DEC_BATCH, 2, M_HEADS, 1, LANES))
        (ht_s,) = _mlstm(p16, pt16, ld, rows, n_seq=DEC_BATCH, seq_len=DEC_SEQ, row0=N_P, layer=l,
                         init=(state_C, st_n, m0))

        merged = _merge(att_p, att_s, ht_p, ht_s, p16, pt16,
                        w_branch16.reshape(N_BRANCH, BRANCH_W, D_MODEL), mnorm_rep,
                        jnp.pad(conv_w[l], ((0, 5), (0, 0))))
        x, h2 = _outproj(merged, w_out16, x, vec(norm_post1[l]), g1,
                         vec(norm_pre2[l]), sc2, sh2)
        if l + 1 < DEPTH:
            nxt = (vec(norm_pre1[l + 1]), mod[l + 1, 1], mod[l + 1, 0])
            x, h = _ffn(h2, w_ffn_in16, w_ffn_out16, x, vec(norm_post2[l]), g2, nxt)
        else:
            y_p, y_s = _ffn(h2, w_ffn_in16, w_ffn_out16, x, vec(norm_post2[l]), g2)

    new_k, new_v = cache
    c_fin, n_fin, m_fin = state
    return (y_p.reshape(BATCH, SEQ, D_MODEL), y_s.reshape(DEC_BATCH, DEC_SEQ, D_MODEL),
            new_k.reshape(BATCH, DEPTH, SEQ, ATT_KV_HEADS, HEAD_DIM),
            new_v.reshape(BATCH, DEPTH, SEQ, ATT_KV_HEADS, HEAD_DIM),
            c_fin, n_fin.reshape(BATCH, DEPTH, 2, M_HEADS, M_DK), m_fin[:, :, :, :, 0, 0])
```

```python
import functools

import jax
import jax.numpy as jnp
from jax import lax
from jax.experimental import pallas as pl
from jax.experimental.pallas import tpu as pltpu

F32 = jnp.float32
BF16 = jnp.bfloat16

D_MODEL = 2048
BATCH = 16
SEQ = 256
DEPTH = 2
DEC_BATCH = 2
DEC_SEQ = 2048
PAST_LEN = 256
GRID_W = 64
EPS = 1e-6
HEAD_DIM = 128
ATT_Q_HEADS = 8
ATT_KV_HEADS = 2
ATT_GROUP = ATT_Q_HEADS // ATT_KV_HEADS
ATT_WIDTH = ATT_Q_HEADS * HEAD_DIM
KV_WIDTH = ATT_KV_HEADS * HEAD_DIM
ROPE_THETA = 10000.0
ROPE_FREQS = HEAD_DIM // 4
M_HEADS = 4
M_DK = 256
M_DV = 256
M_WIDTH = M_HEADS * M_DV
M_CHUNK = 128
CONV_WIDTH = 1024
N_BRANCH = 3
BRANCH_W = 1024
FF_HIDDEN = 5632
IN_WIDTH = 14864

LANES = 128
N_P = BATCH * SEQ
N_S = DEC_BATCH * DEC_SEQ
N_TOK = N_P + N_S
N_CHUNKS = N_TOK // M_CHUNK
N_COND = 8
KV_LEN_S = PAST_LEN + DEC_SEQ

W_AQ, W_AK, W_AV = 0, 1024, 1280
W_MQ, W_MV, W_MG, W_CB = 1536, 3584, 5632, 5648
GATE_COLS = 4 * M_HEADS
PROJ_TN = 512
N_TILE_A = W_MG // PROJ_TN
N_TILE_C = (IN_WIDTH - W_CB) // PROJ_TN
N_TILE_32 = W_MQ // PROJ_TN
TILE_T0 = W_MV // PROJ_TN
TILE_T1 = N_TILE_A
TILE_MO = TILE_T0 + M_WIDTH // PROJ_TN
TILE_GL = N_TILE_A + 3 * CONV_WIDTH // PROJ_TN

P32_W = W_MQ
C16_GL, C16_MQ, C16_MK = 0, 6144, 7168
C16_CB, C16_CC, C16_CX = 8192, 9216, 10240
P16_W = 11264
RT_MV, RT_MO = 0, 1024
PT_H = 2048
M_AUG = M_DV + LANES
GATE_ROWS = 24

MOD_W = 6 * D_MODEL
MOD_TN = 128

NT_DIMS = (((1,), (1,)), ((), ()))
TN_DIMS = (((0,), (0,)), ((), ()))


def _cparams(semantics, vmem_mb):
    return pltpu.CompilerParams(dimension_semantics=semantics,
                                vmem_limit_bytes=vmem_mb * 1024 * 1024)


def _rms(x):
    return x * lax.rsqrt(jnp.mean(x * x, axis=-1, keepdims=True) + EPS)


def _twice_sigmoid_of_twice(x_half):
    return jnp.tanh(x_half) + 1.0


def _sigmoid(x):
    return 0.5 * _twice_sigmoid_of_twice(0.5 * x)


def _cond_row(tile, tm):
    return jnp.where(tile < N_P // tm, 0, 1 + (tile * tm - N_P) // DEC_SEQ)


def _mod_kernel(c_ref, w_ref, b_ref, o_ref):
    c = c_ref[...]
    a = (c * _sigmoid(c)).astype(BF16)
    o_ref[...] = jnp.dot(a, w_ref[...].astype(BF16), preferred_element_type=F32) + b_ref[...]


def _modulation(cond, w_mod, b_mod, layer):
    tn = 1024
    return pl.pallas_call(
        _mod_kernel,
        grid=(MOD_W // tn,),
        in_specs=[
            pl.BlockSpec((N_COND, D_MODEL), lambda j: (0, 0)),
            pl.BlockSpec((None, D_MODEL, tn), lambda j: (layer, 0, j)),
            pl.BlockSpec((None, 1, tn), lambda j: (layer, 0, j)),
        ],
        out_specs=pl.BlockSpec((N_COND, tn), lambda j: (0, j)),
        out_shape=jax.ShapeDtypeStruct((N_COND, MOD_W), F32),
        compiler_params=_cparams(("parallel",), 32),
        name="modulation",
    )(cond, w_mod, b_mod)


def _prenorm_kernel(xp_ref, xs_ref, g_ref, sc_ref, sh_ref, h_ref, *, n_p_tiles):
    def emit(src_ref):
        y = _rms(src_ref[...]) * g_ref[...]
        h_ref[...] = (y * (1 + sc_ref[...]) + sh_ref[...]).astype(BF16)

    i = pl.program_id(0)
    pl.when(i < n_p_tiles)(lambda: emit(xp_ref))
    pl.when(i >= n_p_tiles)(lambda: emit(xs_ref))


def _prenorm(xp, xs, g, sc, sh):
    tm = 512
    npt = N_P // tm
    row = lambda i: (_cond_row(i, tm), 0, 0)
    tile = pl.BlockSpec((tm, D_MODEL), lambda i: (i, 0))
    return pl.pallas_call(
        functools.partial(_prenorm_kernel, n_p_tiles=npt),
        grid=(N_TOK // tm,),
        in_specs=[
            pl.BlockSpec((tm, D_MODEL), lambda i: (jnp.minimum(i, npt - 1), 0)),
            pl.BlockSpec((tm, D_MODEL), lambda i: (jnp.maximum(i - npt, 0), 0)),
            pl.BlockSpec((1, D_MODEL), lambda i: (0, 0)),
            pl.BlockSpec((None, 1, D_MODEL), row),
            pl.BlockSpec((None, 1, D_MODEL), row),
        ],
        out_specs=tile,
        out_shape=jax.ShapeDtypeStruct((N_TOK, D_MODEL), BF16),
        compiler_params=_cparams(("arbitrary",), 32),
        name="prenorm",
    )(xp, xs, g, sc, sh)


def _proj_tile16(j):
    first_gl = N_TILE_A + 3 * CONV_WIDTH // PROJ_TN
    mq0 = C16_MQ // PROJ_TN
    return jnp.where(j < TILE_T0, mq0 + jnp.maximum(j - N_TILE_32, 0),
                     jnp.where(j < TILE_T1, mq0 + TILE_T0 - N_TILE_32 - 1,
                               jnp.where(j < first_gl, C16_CB // PROJ_TN + (j - N_TILE_A),
                                         j - first_gl)))


FIN_ROWS, FOUT_ROWS = 32, 128
N_FIN_BLK = D_MODEL // FIN_ROWS
N_FOUT_BLK = FF_HIDDEN // FOUT_ROWS


def _proj_kernel(*refs, next_mod):
    x_ref, w_ref, tail_ref, fin_ref, fout_ref = refs[:5]
    if next_mod:
        cond_ref, wm_ref, bm_ref = refs[5:8]
        o32_ref, o16_ref, ot_ref, fin16_ref, fout16_ref, mod_ref, w_scr = refs[8:]
    else:
        o32_ref, o16_ref, ot_ref, fin16_ref, fout16_ref, w_scr = refs[5:]
    j = pl.program_id(1)
    shifted = j >= N_TILE_A
    off = pl.multiple_of(jnp.where(shifted, GATE_COLS, 0), 8)
    body = PROJ_TN - GATE_COLS
    halved = ((j >= TILE_MO) & (j < TILE_T1)) | (j >= TILE_GL)
    scale = jnp.where(halved, 0.5, 1.0)
    transposed = (j >= TILE_T0) & (j < TILE_T1)

    def prepare():
        w_scr[0:body, :] = (w_ref[pl.ds(off, body), :] * scale).astype(BF16)
        w_scr[body:, :] = (jnp.where(shifted, tail_ref[...], w_ref[body:, :]) * scale).astype(BF16)
        fin16_ref[...] = fin_ref[...].astype(BF16)
        fout16_ref[...] = fout_ref[...].astype(BF16)
        if next_mod:
            c = cond_ref[...]
            a = (c * _sigmoid(c)).astype(BF16)
            mod_ref[...] = jnp.dot(a, wm_ref[...].astype(BF16),
                                   preferred_element_type=F32) + bm_ref[...]

    @pl.when(transposed)
    def _():
        prepare()
        acc_t = lax.dot_general(w_scr[...], x_ref[...], NT_DIMS, preferred_element_type=F32)
        ot_ref[...] = acc_t.astype(BF16)

    @pl.when(j < N_TILE_32)
    def _():
        prepare()
        o32_ref[...] = lax.dot_general(x_ref[...], w_scr[...], NT_DIMS, preferred_element_type=F32)

    @pl.when((j >= N_TILE_32) & jnp.logical_not(transposed))
    def _():
        prepare()
        acc = lax.dot_general(x_ref[...], w_scr[...], NT_DIMS, preferred_element_type=F32)
        o16_ref[...] = acc.astype(BF16)


def _projection(h, w_t, w_ffn_in, w_ffn_out, layer, next_mod=None):
    tm = 2048
    nj = N_TILE_A + N_TILE_C
    n_steps = (N_TOK // tm) * nj
    tails_per_tile = PROJ_TN // GATE_COLS
    tail0 = W_MG // GATE_COLS
    assert n_steps >= N_FIN_BLK + N_FOUT_BLK and n_steps >= MOD_W // MOD_TN

    def tail_idx(i, j):
        return (layer, jnp.where(j < N_TILE_A, tail0, (j + 1) * tails_per_tile), 0)

    fin_blk = lambda i, j: jnp.minimum(i * nj + j, N_FIN_BLK - 1)
    fout_blk = lambda i, j: jnp.clip(i * nj + j - N_FIN_BLK, 0, N_FOUT_BLK - 1)
    mod_blk = lambda i, j: jnp.minimum(i * nj + j, MOD_W // MOD_TN - 1)
    in_specs = [
        pl.BlockSpec((tm, D_MODEL), lambda i, j: (i, 0), pipeline_mode=pl.Buffered(1)),
        pl.BlockSpec((None, PROJ_TN, D_MODEL), lambda i, j: (layer, j, 0)),
        pl.BlockSpec((None, GATE_COLS, D_MODEL), tail_idx),
        pl.BlockSpec((None, FIN_ROWS, 2 * FF_HIDDEN), lambda i, j: (layer, fin_blk(i, j), 0)),
        pl.BlockSpec((None, FOUT_ROWS, D_MODEL), lambda i, j: (layer, fout_blk(i, j), 0)),
    ]
    args = [h, w_t, w_t, w_ffn_in, w_ffn_out]
    out_specs = [
        pl.BlockSpec((tm, PROJ_TN), lambda i, j: (i, jnp.minimum(j, N_TILE_32 - 1))),
        pl.BlockSpec((tm, PROJ_TN), lambda i, j: (i, _proj_tile16(j))),
        pl.BlockSpec((PROJ_TN, tm),
                     lambda i, j: (jnp.clip(j - TILE_T0, 0, TILE_T1 - TILE_T0 - 1), i)),
        pl.BlockSpec((FIN_ROWS, 2 * FF_HIDDEN), lambda i, j: (fin_blk(i, j), 0)),
        pl.BlockSpec((FOUT_ROWS, D_MODEL), lambda i, j: (fout_blk(i, j), 0)),
    ]
    out_shape = [jax.ShapeDtypeStruct((N_TOK, P32_W), F32),
                 jax.ShapeDtypeStruct((N_TOK, P16_W), BF16),
                 jax.ShapeDtypeStruct((PT_H, N_TOK), BF16),
                 jax.ShapeDtypeStruct((D_MODEL, 2 * FF_HIDDEN), BF16),
                 jax.ShapeDtypeStruct((FF_HIDDEN, D_MODEL), BF16)]
    if next_mod is not None:
        in_specs += [
            pl.BlockSpec((N_COND, D_MODEL), lambda i, j: (0, 0)),
            pl.BlockSpec((None, D_MODEL, MOD_TN), lambda i, j: (layer + 1, 0, mod_blk(i, j))),
            pl.BlockSpec((None, 1, MOD_TN), lambda i, j: (layer + 1, 0, mod_blk(i, j))),
        ]
        args += list(next_mod)
        out_specs.append(pl.BlockSpec((N_COND, MOD_TN), lambda i, j: (0, mod_blk(i, j))))
        out_shape.append(jax.ShapeDtypeStruct((N_COND, MOD_W), F32))
    return pl.pallas_call(
        functools.partial(_proj_kernel, next_mod=next_mod is not None),
        grid=(N_TOK // tm, nj),
        in_specs=in_specs,
        out_specs=out_specs,
        out_shape=out_shape,
        scratch_shapes=[pltpu.VMEM((PROJ_TN, D_MODEL), BF16)],
        compiler_params=_cparams(("arbitrary", "arbitrary"), 56),
        name="projection",
    )(*args)


def _gate_kernel(h_ref, w_ref, b_ref, wb_ref, wo_ref, ld_ref, row_ref, wb16_ref, wo16_ref,
                 *, chunks):
    wb16_ref[...] = wb_ref[...].astype(BF16)
    wo16_ref[...] = wo_ref[...].astype(BF16)
    g = lax.dot_general(h_ref[...], w_ref[...].astype(BF16), NT_DIMS,
                        preferred_element_type=F32) + b_ref[...]
    lf = jax.nn.log_sigmoid(g)
    s_idx = lax.broadcasted_iota(jnp.int32, (M_CHUNK, M_CHUNK), 0)
    t_idx = lax.broadcasted_iota(jnp.int32, (M_CHUNK, M_CHUNK), 1)
    tril = (t_idx <= s_idx).astype(BF16)
    lane = lax.broadcasted_iota(jnp.int32, (M_CHUNK, LANES), 1)
    lf_hi = lf.astype(BF16)
    rest = lf - lf_hi.astype(F32)
    lf_mid = rest.astype(BF16)
    lf_lo = (rest - lf_mid.astype(F32)).astype(BF16)
    for ch in range(chunks):
        rows = slice(ch * M_CHUNK, (ch + 1) * M_CHUNK)
        gc = g[rows]
        lfc = lf[rows]
        pre = (jnp.dot(tril, lf_hi[rows], preferred_element_type=F32)
               + jnp.dot(tril, lf_mid[rows], preferred_element_type=F32)
               + jnp.dot(tril, lf_lo[rows], preferred_element_type=F32))
        tot = jnp.broadcast_to(pre[M_CHUNK - 1:M_CHUNK, :], pre.shape)
        suf = tot - pre + lfc
        both = jnp.where((lane >= 4) & (lane < 8), pre,
                         jnp.where((lane >= 12) & (lane < 16), suf, gc))
        both_t = both.T
        tot_t = tot.T
        for d in range(2):
            mask = (s_idx <= t_idx) if d == 0 else (s_idx >= t_idx)
            row_ref[d, ch, 20:GATE_ROWS, :] = jnp.zeros((GATE_ROWS - 20, M_CHUNK), F32)
            for hd in range(M_HEADS):
                li, lb = 8 * d + hd, 8 * d + 4 + hd
                i_row = both_t[li:li + 1, :]
                b_row = both_t[lb:lb + 1, :]
                bl_row = tot_t[lb:lb + 1, :]
                key = both[:, lb:lb + 1] - both[:, li:li + 1]
                ld = jnp.where(mask, b_row - key, -jnp.inf)
                wl = bl_row - b_row + i_row
                ld_ref[d, hd, rows, :] = ld
                row_ref[d, ch, hd:hd + 1, :] = i_row
                row_ref[d, ch, 4 + hd:5 + hd, :] = b_row
                row_ref[d, ch, 8 + hd:9 + hd, :] = jnp.max(ld, axis=0, keepdims=True)
                row_ref[d, ch, 12 + hd:13 + hd, :] = bl_row
                row_ref[d, ch, 16 + hd:17 + hd, :] = jnp.broadcast_to(
                    jnp.max(wl, axis=-1, keepdims=True), (1, M_CHUNK))


def _gates(h, w_t, layer, bias, w_branch, w_out):
    tm = 512
    chunks = tm // M_CHUNK
    steps = N_TOK // tm
    wb_rows = N_BRANCH * BRANCH_W // steps
    wo_rows = D_MODEL // steps
    return pl.pallas_call(
        functools.partial(_gate_kernel, chunks=chunks),
        grid=(steps,),
        in_specs=[
            pl.BlockSpec((tm, D_MODEL), lambda i: (i, 0)),
            pl.BlockSpec((None, LANES, D_MODEL), lambda i: (layer, W_MG // LANES, 0)),
            pl.BlockSpec((1, LANES), lambda i: (0, 0)),
            pl.BlockSpec((None, wb_rows, D_MODEL), lambda i: (layer, i, 0)),
            pl.BlockSpec((None, wo_rows, D_MODEL), lambda i: (layer, i, 0)),
        ],
        out_specs=[
            pl.BlockSpec((2, M_HEADS, tm, M_CHUNK), lambda i: (0, 0, i, 0)),
            pl.BlockSpec((2, chunks, GATE_ROWS, M_CHUNK), lambda i: (0, i, 0, 0)),
            pl.BlockSpec((wb_rows, D_MODEL), lambda i: (i, 0)),
            pl.BlockSpec((wo_rows, D_MODEL), lambda i: (i, 0)),
        ],
        out_shape=[
            jax.ShapeDtypeStruct((2, M_HEADS, N_TOK, M_CHUNK), F32),
            jax.ShapeDtypeStruct((2, N_CHUNKS, GATE_ROWS, M_CHUNK), F32),
            jax.ShapeDtypeStruct((N_BRANCH * BRANCH_W, D_MODEL), BF16),
            jax.ShapeDtypeStruct((D_MODEL, D_MODEL), BF16),
        ],
        compiler_params=_cparams(("arbitrary",), 32),
        name="mlstm_gates",
    )(h, w_t, bias, w_branch.reshape(DEPTH, N_BRANCH * BRANCH_W, D_MODEL), w_out)


def _mlstm_kernel(*refs, n_par, has_init, emit_state, has_prev):
    pos = 0
    seq_refs = []
    for _ in range(n_par):
        seq_refs.append(refs[pos:pos + 5])
        pos += 5
    if has_init:
        c0_ref, n0_ref, m0_ref = refs[pos:pos + 3]
        pos += 3
    if has_prev:
        pos += 3
    h_ref = refs[pos]
    pos += 1
    if emit_state:
        cout_ref, nout_ref, mout_ref = refs[pos:pos + 3]
        pos += 3
    c_scr, m_scr = refs[pos:pos + 2]

    ci = pl.program_id(2)
    last = pl.num_programs(2) - 1
    pad_rows = lax.broadcasted_iota(jnp.int32, (LANES, M_CHUNK), 0)
    one_row = (pad_rows == 0).astype(F32)

    @pl.when(ci == 0)
    def _():
        if has_init:
            for u in range(n_par):
                for hd in range(M_HEADS):
                    c_scr[u, hd, 0:M_DV, :] = c0_ref[u, hd].T
                    pad = lax.broadcasted_iota(jnp.int32, (LANES, M_DK), 0)
                    c_scr[u, hd, M_DV:M_AUG, :] = jnp.where(pad == 0, n0_ref[u, hd], 0.0)
            m_scr[...] = m0_ref[...]
        else:
            c_scr[...] = jnp.zeros_like(c_scr)
            m_scr[...] = jnp.zeros_like(m_scr)

    for u in range(n_par):
        q_ref, k_ref, vt_ref, ld_ref, row_ref = seq_refs[u]
        for hd in range(M_HEADS):
            cols = slice(hd * M_DK, (hd + 1) * M_DK)
            q = q_ref[:, cols]
            k = k_ref[:, cols] * (M_DK ** -0.5)
            vt_aug = jnp.concatenate([vt_ref[cols, :].astype(F32), one_row], axis=0)
            ld = ld_ref[hd]
            i_row = row_ref[hd:hd + 1, :]
            b_row = row_ref[4 + hd:5 + hd, :]
            ldmax_row = row_ref[8 + hd:9 + hd, :]
            bl_row = row_ref[12 + hd:13 + hd, :]
            wmax_row = row_ref[16 + hd:17 + hd, :]
            mem = c_scr[u, hd]
            m_prev = m_scr[u, hd]

            g_row = b_row + m_prev
            mt_row = jnp.maximum(g_row, ldmax_row)
            kq = lax.dot_general(k, q, NT_DIMS, preferred_element_type=F32)
            st = (kq * jnp.exp(ld - mt_row)).astype(BF16)
            nd = jnp.exp(g_row - mt_row) * lax.dot_general(
                mem.astype(BF16), q, NT_DIMS, preferred_element_type=F32) \
                + jnp.dot(vt_aug.astype(BF16), st, preferred_element_type=F32)
            den_row = nd[M_DV:M_DV + 1, :]
            scale_row = 1.0 / jnp.maximum(jnp.abs(den_row), jnp.exp(-mt_row))
            h_ref[u, cols, :] = nd[0:M_DV, :] * scale_row

            m_new = jnp.maximum(bl_row + m_prev, wmax_row)
            w_row = jnp.exp(bl_row - b_row + i_row - m_new)
            dec = jnp.exp(bl_row + m_prev - m_new)
            dec_wide = jnp.concatenate([dec] * (M_DK // LANES), axis=1)
            c_scr[u, hd] = dec_wide * mem + jnp.dot((vt_aug * w_row).astype(BF16), k,
                                                    preferred_element_type=F32)
            m_scr[u, hd] = m_new

    if emit_state:
        @pl.when(ci == last)
        def _():
            if has_prev:
                slot = lambda ref: ref
            else:
                slot = lambda ref: ref.at[:, 0]
                for ref in (cout_ref, nout_ref, mout_ref):
                    for l in range(1, DEPTH):
                        ref[:, l] = jnp.zeros(ref.shape[:1] + ref.shape[2:], F32)
            for u in range(n_par):
                for hd in range(M_HEADS):
                    slot(cout_ref)[u, hd] = c_scr[u, hd, 0:M_DV, :].T
                    slot(nout_ref)[u, hd] = c_scr[u, hd, M_DV:M_DV + 1, :]
            slot(mout_ref)[...] = m_scr[...]


def _mlstm(p16, pt16, ld, rows, *, n_seq, seq_len, row0, layer, init=None, prev_state=None,
           emit_state=False):
    n_par = min(n_seq, 4)
    nc = seq_len // M_CHUNK
    blk0 = row0 // M_CHUNK

    def chunk_of(c, d):
        return jnp.where(d == 0, c, nc - 1 - c)

    in_specs, args = [], []
    for u in range(n_par):
        blk = lambda g, d, c, u=u: blk0 + (g * n_par + u) * nc + chunk_of(c, d)
        in_specs += [
            pl.BlockSpec((M_CHUNK, M_WIDTH), lambda g, d, c, blk=blk: (blk(g, d, c), C16_MQ // M_WIDTH)),
            pl.BlockSpec((M_CHUNK, M_WIDTH), lambda g, d, c, blk=blk: (blk(g, d, c), C16_MK // M_WIDTH)),
            pl.BlockSpec((M_WIDTH, M_CHUNK), lambda g, d, c, blk=blk: (RT_MV // M_WIDTH, blk(g, d, c))),
            pl.BlockSpec((None, M_HEADS, M_CHUNK, M_CHUNK), lambda g, d, c, blk=blk: (d, 0, blk(g, d, c), 0)),
            pl.BlockSpec((None, None, GATE_ROWS, M_CHUNK), lambda g, d, c, blk=blk: (d, blk(g, d, c), 0, 0)),
        ]
        args += [p16, p16, pt16, ld, rows]
    if init is not None:
        in_specs += [
            pl.BlockSpec((n_par, None, None, M_HEADS, M_DK, M_DV), lambda g, d, c: (g, layer, d, 0, 0, 0)),
            pl.BlockSpec((n_par, None, None, M_HEADS, 1, M_DK), lambda g, d, c: (g, layer, d, 0, 0, 0)),
            pl.BlockSpec((n_par, None, M_HEADS, 1, LANES), lambda g, d, c: (g, d, 0, 0, 0)),
        ]
        args += list(init)
    aliases = {}
    if prev_state is not None:
        for k_, a in enumerate(prev_state):
            aliases[len(args)] = 1 + k_
            in_specs.append(pl.BlockSpec(memory_space=pl.ANY))
            args.append(a)
    out_specs = [pl.BlockSpec((None, n_par, None, M_WIDTH, M_CHUNK),
                              lambda g, d, c: (d, g, chunk_of(c, d), 0, 0))]
    out_shape = [jax.ShapeDtypeStruct((2, n_seq, nc, M_WIDTH, M_CHUNK), F32)]
    if emit_state:
        if prev_state is None:
            lblk, lidx = DEPTH, 0
        else:
            lblk, lidx = None, layer
        out_specs += [
            pl.BlockSpec((n_par, lblk, None, M_HEADS, M_DK, M_DV), lambda g, d, c: (g, lidx, d, 0, 0, 0)),
            pl.BlockSpec((n_par, lblk, None, M_HEADS, 1, M_DK), lambda g, d, c: (g, lidx, d, 0, 0, 0)),
            pl.BlockSpec((n_par, lblk, None, M_HEADS, 1, LANES), lambda g, d, c: (g, lidx, d, 0, 0, 0)),
        ]
        out_shape += [
            jax.ShapeDtypeStruct((n_seq, DEPTH, 2, M_HEADS, M_DK, M_DV), F32),
            jax.ShapeDtypeStruct((n_seq, DEPTH, 2, M_HEADS, 1, M_DK), F32),
            jax.ShapeDtypeStruct((n_seq, DEPTH, 2, M_HEADS, 1, LANES), F32),
        ]
    return pl.pallas_call(
        functools.partial(_mlstm_kernel, n_par=n_par, has_init=init is not None,
                          emit_state=emit_state, has_prev=prev_state is not None),
        grid=(n_seq // n_par, 2, nc),
        in_specs=in_specs,
        out_specs=out_specs,
        out_shape=out_shape,
        input_output_aliases=aliases,
        scratch_shapes=[
            pltpu.VMEM((n_par, M_HEADS, M_AUG, M_DK), F32),
            pltpu.VMEM((n_par, M_HEADS, 1, LANES), F32),
        ],
        compiler_params=_cparams(("arbitrary", "arbitrary", "arbitrary"), 40),
        name="mlstm_scan_dec" if init is not None else "mlstm_scan_ctx",
    )(*args)


def _swap32(x):
    lane = lax.broadcasted_iota(jnp.int32, x.shape, 1)
    return jnp.where((lane % 64) < 32, pltpu.roll(x, HEAD_DIM - 32, axis=1),
                     pltpu.roll(x, 32, axis=1))


def _group_attention(q_heads, kb, vb, o_ref):
    for g, q in enumerate(q_heads):
        s = lax.dot_general(q.astype(BF16), kb, NT_DIMS, preferred_element_type=F32)
        e = jnp.exp2(s - jnp.max(s, axis=-1, keepdims=True))
        l = jnp.sum(e, axis=-1, keepdims=True)
        o = jnp.dot(e.astype(BF16), vb, preferred_element_type=F32) / l
        o_ref[:, g * HEAD_DIM:(g + 1) * HEAD_DIM] = o.astype(BF16)


Q_SCALE = HEAD_DIM ** -0.5 * 1.4426950408889634


def _attn_ctx_kernel(*refs, first_layer):
    q_ref, k_ref, v_ref, qw_ref, kw_ref = refs[:5]
    o_ref, kout_ref, vout_ref = refs[-3:]
    if first_layer:
        k_dst, v_dst = kout_ref.at[0], vout_ref.at[0]
        for l in range(1, DEPTH):
            kout_ref[l] = jnp.zeros(kout_ref.shape[1:], F32)
            vout_ref[l] = jnp.zeros(vout_ref.shape[1:], F32)
    else:
        k_dst, v_dst = kout_ref, vout_ref
    gw = ATT_GROUP * HEAD_DIM
    for kv in range(ATT_KV_HEADS):
        kv_cols = slice(kv * HEAD_DIM, (kv + 1) * HEAD_DIM)
        k = _rms(k_ref[:, kv_cols]) * kw_ref[...]
        v = v_ref[:, kv_cols]
        k_dst[:, kv, :] = k
        v_dst[:, kv, :] = v
        q_heads = [_rms(q_ref[:, kv * gw + g * HEAD_DIM:kv * gw + (g + 1) * HEAD_DIM])
                   * qw_ref[...] * Q_SCALE for g in range(ATT_GROUP)]
        _group_attention(q_heads, k.astype(BF16), v.astype(BF16),
                         o_ref.at[:, kv * gw:(kv + 1) * gw])


def _attention_ctx(p32, qw, kw, layer, prev_cache=None):
    in_specs = [
        pl.BlockSpec((SEQ, ATT_WIDTH), lambda b: (b, W_AQ // ATT_WIDTH)),
        pl.BlockSpec((SEQ, KV_WIDTH), lambda b: (b, W_AK // KV_WIDTH)),
        pl.BlockSpec((SEQ, KV_WIDTH), lambda b: (b, W_AV // KV_WIDTH)),
        pl.BlockSpec((1, HEAD_DIM), lambda b: (0, 0)),
        pl.BlockSpec((1, HEAD_DIM), lambda b: (0, 0)),
    ]
    args = [p32, p32, p32, qw, kw]
    aliases = {}
    if prev_cache is not None:
        for k_, a in enumerate(prev_cache):
            aliases[len(args)] = 1 + k_
            in_specs.append(pl.BlockSpec(memory_space=pl.ANY))
            args.append(a)
    if prev_cache is None:
        cache_spec = pl.BlockSpec((None, DEPTH, SEQ, ATT_KV_HEADS, HEAD_DIM),
                                  lambda b: (b, 0, 0, 0, 0))
    else:
        cache_spec = pl.BlockSpec((None, None, SEQ, ATT_KV_HEADS, HEAD_DIM),
                                  lambda b: (b, layer, 0, 0, 0))
    cache_shape = jax.ShapeDtypeStruct((BATCH, DEPTH, SEQ, ATT_KV_HEADS, HEAD_DIM), F32)
    return pl.pallas_call(
        functools.partial(_attn_ctx_kernel, first_layer=prev_cache is None),
        grid=(BATCH,),
        in_specs=in_specs,
        out_specs=[pl.BlockSpec((SEQ, ATT_WIDTH), lambda b: (b, 0)), cache_spec, cache_spec],
        out_shape=[jax.ShapeDtypeStruct((N_P, ATT_WIDTH), BF16), cache_shape, cache_shape],
        input_output_aliases=aliases,
        compiler_params=_cparams(("arbitrary",), 32),
        name="attention_ctx",
    )(*args)


def _attn_dec_kernel(q_ref, k_ref, v_ref, ck_ref, cv_ref, qw_ref, kw_ref,
                     cos_ref, sin_ref, cosq_ref, sinq_ref, o_ref, k_scr, v_scr):
    @pl.when(pl.program_id(2) == 0)
    def _():
        k_scr[0:PAST_LEN, :] = ck_ref[...].astype(BF16)
        v_scr[0:PAST_LEN, :] = cv_ref[...].astype(BF16)
        k = _rms(k_ref[...]) * kw_ref[...]
        k = k * cos_ref[...] + _swap32(k) * sin_ref[...]
        k_scr[PAST_LEN:, :] = k.astype(BF16)
        v_scr[PAST_LEN:, :] = v_ref[...].astype(BF16)

    q_heads = []
    for g in range(ATT_GROUP):
        q = _rms(q_ref[:, g * HEAD_DIM:(g + 1) * HEAD_DIM]) * qw_ref[...]
        q_heads.append((q * cosq_ref[...] + _swap32(q) * sinq_ref[...]) * Q_SCALE)
    _group_attention(q_heads, k_scr[...], v_scr[...], o_ref)


def _attention_dec(p32, ck, cv, qw, kw, cos, sin, layer):
    tq = 256
    gw = ATT_GROUP * HEAD_DIM
    nq = DEC_SEQ // tq
    seq0 = N_P // DEC_SEQ
    cache_spec = pl.BlockSpec((None, None, PAST_LEN, HEAD_DIM), lambda b, h, i: (b, layer, 0, h))
    return pl.pallas_call(
        _attn_dec_kernel,
        grid=(DEC_BATCH, ATT_KV_HEADS, nq),
        in_specs=[
            pl.BlockSpec((tq, gw), lambda b, h, i: (N_P // tq + b * nq + i, W_AQ // gw + h)),
            pl.BlockSpec((DEC_SEQ, HEAD_DIM), lambda b, h, i: (seq0 + b, W_AK // HEAD_DIM + h)),
            pl.BlockSpec((DEC_SEQ, HEAD_DIM), lambda b, h, i: (seq0 + b, W_AV // HEAD_DIM + h)),
            cache_spec, cache_spec,
            pl.BlockSpec((1, HEAD_DIM), lambda b, h, i: (0, 0)),
            pl.BlockSpec((1, HEAD_DIM), lambda b, h, i: (0, 0)),
            pl.BlockSpec((DEC_SEQ, HEAD_DIM), lambda b, h, i: (0, 0)),
            pl.BlockSpec((DEC_SEQ, HEAD_DIM), lambda b, h, i: (0, 0)),
            pl.BlockSpec((tq, HEAD_DIM), lambda b, h, i: (i, 0)),
            pl.BlockSpec((tq, HEAD_DIM), lambda b, h, i: (i, 0)),
        ],
        out_specs=pl.BlockSpec((tq, gw), lambda b, h, i: (b * nq + i, h)),
        out_shape=jax.ShapeDtypeStruct((N_S, ATT_WIDTH), BF16),
        scratch_shapes=[
            pltpu.VMEM((KV_LEN_S, HEAD_DIM), BF16),
            pltpu.VMEM((KV_LEN_S, HEAD_DIM), BF16),
        ],
        compiler_params=_cparams(("arbitrary", "arbitrary", "arbitrary"), 32),
        name="attention_dec",
    )(p32, p32, p32, ck, cv, qw, kw, cos, sin, cos, sin)


def _rope_tables():
    t = jnp.arange(DEC_SEQ)
    inv = ROPE_THETA ** (-jnp.arange(ROPE_FREQS, dtype=F32) / ROPE_FREQS)
    ang = jnp.stack([t // GRID_W, t % GRID_W], axis=-1).astype(F32)[:, :, None] * inv
    cos = jnp.cos(ang)
    sin = jnp.sin(ang)
    cos = jnp.stack([cos, cos], axis=2).reshape(DEC_SEQ, HEAD_DIM)
    sin = jnp.stack([-sin, sin], axis=2).reshape(DEC_SEQ, HEAD_DIM)
    return cos, sin


def _merge_kernel(attp_ref, atts_ref, hfp_ref, hbp_ref, hfs_ref, hbs_ref,
                  mot_ref, cb_ref, cc_ref, cx_ref, ccp_ref, cxp_ref, ccn_ref, cxn_ref,
                  gl0_ref, gl1_ref, gl2_ref, wb_ref, mn_ref, cw_ref, o_ref, att_scr, hsum_scr,
                  *, tm, halo):
    i = pl.program_id(0)
    row0 = i * tm
    in_p = row0 < N_P
    off = jnp.where(in_p, row0 % SEQ, (row0 - N_P) % DEC_SEQ)
    seq_len = jnp.where(in_p, SEQ, DEC_SEQ)
    first = off == 0
    final = off + tm == seq_len

    @pl.when(in_p)
    def _():
        att_scr[...] = attp_ref[...]
        hsum_scr[...] = hfp_ref[...] + hbp_ref[...]

    @pl.when(jnp.logical_not(in_p))
    def _():
        att_scr[...] = atts_ref[...]
        hsum_scr[...] = hfs_ref[...] + hbs_ref[...]

    mn_half = 0.5 * mn_ref[...]
    chunk_parts = []
    for ch in range(tm // M_CHUNK):
        parts = []
        for hd in range(M_HEADS):
            x = hsum_scr[ch, hd * M_DV:(hd + 1) * M_DV, :]
            parts.append(x * lax.rsqrt(jnp.mean(x * x, axis=0, keepdims=True) + EPS))
        chunk_parts.append(jnp.concatenate(parts, axis=0) * mn_half)
    ml_t = jnp.concatenate(chunk_parts, axis=1) * _twice_sigmoid_of_twice(
        mot_ref[...].astype(F32))

    u = cc_ref[...].astype(F32) * cx_ref[...].astype(F32)
    u_halo_prev = ccp_ref[...].astype(F32) * cxp_ref[...].astype(F32)
    u_halo_next = ccn_ref[...].astype(F32) * cxn_ref[...].astype(F32)
    u_prev_row = jnp.where(first, 0.0, u_halo_prev[halo - 1:halo, :])
    u_next_row = jnp.where(final, 0.0, u_halo_next[0:1, :])
    ridx = lax.broadcasted_iota(jnp.int32, u.shape, 0)
    u_prev = jnp.where(ridx == 0, u_prev_row, pltpu.roll(u, 1, axis=0))
    u_next = jnp.where(ridx == tm - 1, u_next_row, pltpu.roll(u, tm - 1, axis=0))
    cw = cw_ref[...]
    cv = cb_ref[...].astype(F32) * (cw[0:1, :] * u_prev + cw[1:2, :] * u + cw[2:3, :] * u_next)

    acc = _twice_sigmoid_of_twice(gl0_ref[...].astype(F32)) * jnp.dot(
        att_scr[...], wb_ref[0], preferred_element_type=F32)
    acc += _twice_sigmoid_of_twice(gl1_ref[...].astype(F32)) * lax.dot_general(
        ml_t.astype(BF16), wb_ref[1], TN_DIMS, preferred_element_type=F32)
    acc += _twice_sigmoid_of_twice(gl2_ref[...].astype(F32)) * jnp.dot(
        cv.astype(BF16), wb_ref[2], preferred_element_type=F32)
    o_ref[...] = (0.5 * acc).astype(BF16)


def _merge(att_p, att_s, ht_p, ht_s, p16, pt16, wb, mnorm_rep, convw):
    tm = 2 * M_CHUNK
    cpt = tm // M_CHUNK
    halo = 16
    nt = N_TOK // tm
    npt = N_P // tm
    th = tm // halo
    last_h = N_TOK // halo - 1
    p_idx = lambda i: jnp.minimum(i, npt - 1)
    s_idx = lambda i: jnp.maximum(i - npt, 0)
    cblk = lambda col: (lambda i: (i, col // BRANCH_W))
    prevh = lambda col: (lambda i: (jnp.maximum(i * th - 1, 0), col // BRANCH_W))
    nexth = lambda col: (lambda i: (jnp.minimum((i + 1) * th, last_h), col // BRANCH_W))
    glblk = lambda g: (lambda i: (i, C16_GL // D_MODEL + g))
    ht_p = ht_p.reshape(2, N_P // M_CHUNK, M_WIDTH, M_CHUNK)
    ht_s = ht_s.reshape(2, N_S // M_CHUNK, M_WIDTH, M_CHUNK)
    ht_spec = lambda d, idx: pl.BlockSpec((None, cpt, M_WIDTH, M_CHUNK), lambda i: (d, idx(i), 0, 0))
    return pl.pallas_call(
        functools.partial(_merge_kernel, tm=tm, halo=halo),
        grid=(nt,),
        in_specs=[
            pl.BlockSpec((tm, BRANCH_W), lambda i: (p_idx(i), 0)),
            pl.BlockSpec((tm, BRANCH_W), lambda i: (s_idx(i), 0)),
            ht_spec(0, p_idx), ht_spec(1, p_idx), ht_spec(0, s_idx), ht_spec(1, s_idx),
            pl.BlockSpec((M_WIDTH, tm), lambda i: (RT_MO // M_WIDTH, i)),
            pl.BlockSpec((tm, BRANCH_W), cblk(C16_CB)),
            pl.BlockSpec((tm, BRANCH_W), cblk(C16_CC)),
            pl.BlockSpec((tm, BRANCH_W), cblk(C16_CX)),
            pl.BlockSpec((halo, BRANCH_W), prevh(C16_CC)),
            pl.BlockSpec((halo, BRANCH_W), prevh(C16_CX)),
            pl.BlockSpec((halo, BRANCH_W), nexth(C16_CC)),
            pl.BlockSpec((halo, BRANCH_W), nexth(C16_CX)),
            pl.BlockSpec((tm, D_MODEL), glblk(0)),
            pl.BlockSpec((tm, D_MODEL), glblk(1)),
            pl.BlockSpec((tm, D_MODEL), glblk(2)),
            pl.BlockSpec((N_BRANCH, BRANCH_W, D_MODEL), lambda i: (0, 0, 0),
                         pipeline_mode=pl.Buffered(1)),
            pl.BlockSpec((M_WIDTH, LANES), lambda i: (0, 0)),
            pl.BlockSpec((8, CONV_WIDTH), lambda i: (0, 0)),
        ],
        out_specs=pl.BlockSpec((tm, D_MODEL), lambda i: (i, 0)),
        out_shape=jax.ShapeDtypeStruct((N_TOK, D_MODEL), BF16),
        scratch_shapes=[pltpu.VMEM((tm, BRANCH_W), BF16),
                        pltpu.VMEM((cpt, M_WIDTH, M_CHUNK), F32)],
        compiler_params=_cparams(("arbitrary",), 52),
        name="branch_merge",
    )(att_p, att_s, ht_p, ht_p, ht_s, ht_s, pt16, p16, p16, p16, p16, p16, p16, p16,
      p16, p16, p16, wb, mnorm_rep, convw)


def _outproj_kernel(*refs, split_x, n_p_tiles):
    m_ref, w_ref = refs[:2]
    if split_x:
        xp_ref, xs_ref = refs[2:4]
        rest = refs[4:]
    else:
        x_ref = refs[2]
        rest = refs[3:]
    gpost_ref, gate_ref, gpre_ref, sc_ref, sh_ref, x_out_ref, h_out_ref = rest
    in_p = pl.program_id(0) < n_p_tiles
    post_scale = gate_ref[...] * gpost_ref[...]
    pre_scale = gpre_ref[...] * (1 + sc_ref[...])
    group = 128
    for r in range(m_ref.shape[0] // group):
        rows = slice(r * group, (r + 1) * group)
        if split_x:
            x_in = jnp.where(in_p, xp_ref[rows, :], xs_ref[rows, :])
        else:
            x_in = x_ref[rows, :]
        mix = jnp.dot(m_ref[rows, :], w_ref[...], preferred_element_type=F32)
        x = x_in + _rms(mix) * post_scale
        x_out_ref[rows, :] = x
        h_out_ref[rows, :] = (_rms(x) * pre_scale + sh_ref[...]).astype(BF16)


def _outproj(merged, w_out, x, gpost, gate, gpre, sc, sh):
    tm = 512
    npt = N_P // tm
    row = lambda i: (_cond_row(i, tm), 0, 0)
    vec = pl.BlockSpec((1, D_MODEL), lambda i: (0, 0))
    cond = pl.BlockSpec((None, 1, D_MODEL), row)
    tile = pl.BlockSpec((tm, D_MODEL), lambda i: (i, 0))
    split_x = isinstance(x, tuple)
    if split_x:
        x_specs = [pl.BlockSpec((tm, D_MODEL), lambda i: (jnp.minimum(i, npt - 1), 0)),
                   pl.BlockSpec((tm, D_MODEL), lambda i: (jnp.maximum(i - npt, 0), 0))]
        x_args = list(x)
    else:
        x_specs, x_args = [tile], [x]
    return pl.pallas_call(
        functools.partial(_outproj_kernel, split_x=split_x, n_p_tiles=npt),
        grid=(N_TOK // tm,),
        in_specs=[tile,
                  pl.BlockSpec((D_MODEL, D_MODEL), lambda i: (0, 0), pipeline_mode=pl.Buffered(1)),
                  *x_specs, vec, cond, vec, cond, cond],
        out_specs=[tile, tile],
        out_shape=[jax.ShapeDtypeStruct((N_TOK, D_MODEL), F32),
                   jax.ShapeDtypeStruct((N_TOK, D_MODEL), BF16)],
        compiler_params=_cparams(("arbitrary",), 48),
        name="out_projection",
    )(merged, w_out, *x_args, gpost, gate, gpre, sc, sh)


def _ffn_kernel(*refs, emit_next, n_p_tiles):
    h_ref, wg_ref, wu_ref, wo_ref, x_ref, gpost_ref, gate_ref = refs[:7]
    if emit_next:
        gpre_ref, sc_ref, sh_ref, x_out_ref, h_out_ref, acc_ref = refs[7:]
    else:
        yp_ref, ys_ref, acc_ref = refs[7:]

    i = pl.program_id(0)
    j = pl.program_id(1)

    @pl.when(j == 0)
    def _():
        acc_ref[...] = jnp.zeros_like(acc_ref)

    h = h_ref[...]
    gate = jnp.dot(h, wg_ref[...], preferred_element_type=F32)
    up = jnp.dot(h, wu_ref[...], preferred_element_type=F32)
    gate_half = 0.5 * gate
    act = (gate_half * _twice_sigmoid_of_twice(gate_half) * up).astype(BF16)
    acc_ref[...] += jnp.dot(act, wo_ref[...], preferred_element_type=F32)

    group = 64

    def result(rows, post_scale):
        return x_ref[rows, :] + _rms(acc_ref[rows, :]) * post_scale

    def epilogue(emit):
        post_scale = gate_ref[...] * gpost_ref[...]
        for r in range(acc_ref.shape[0] // group):
            rows = slice(r * group, (r + 1) * group)
            emit(rows, result(rows, post_scale))

    is_last = j == pl.num_programs(1) - 1
    if emit_next:
        @pl.when(is_last)
        def _():
            pre_scale = gpre_ref[...] * (1 + sc_ref[...])

            def emit(rows, x):
                x_out_ref[rows, :] = x
                h_out_ref[rows, :] = (_rms(x) * pre_scale + sh_ref[...]).astype(BF16)

            epilogue(emit)
    else:
        def to_prompt(rows, x):
            yp_ref[rows, :] = x

        def to_sample(rows, x):
            ys_ref[rows, :] = x

        pl.when(is_last & (i < n_p_tiles))(lambda: epilogue(to_prompt))
        pl.when(is_last & (i >= n_p_tiles))(lambda: epilogue(to_sample))


def _ffn(h, w_in, w_out, x, gpost, gate, nxt=None):
    tm, th = 512, 512
    nj = FF_HIDDEN // th
    npt = N_P // tm
    row = lambda i, j: (_cond_row(i, tm), 0, 0)
    vec = pl.BlockSpec((1, D_MODEL), lambda i, j: (0, 0))
    cond = pl.BlockSpec((None, 1, D_MODEL), row)
    tile = pl.BlockSpec((tm, D_MODEL), lambda i, j: (i, 0))
    in_specs = [
        tile,
        pl.BlockSpec((D_MODEL, th), lambda i, j: (0, j)),
        pl.BlockSpec((D_MODEL, th), lambda i, j: (0, nj + j)),
        pl.BlockSpec((th, D_MODEL), lambda i, j: (j, 0)),
        tile, vec, cond,
    ]
    args = [h, w_in, w_in, w_out, x, gpost, gate]
    if nxt is not None:
        in_specs += [vec, cond, cond]
        args += list(nxt)
        out_specs = [tile, tile]
        out_shape = [jax.ShapeDtypeStruct((N_TOK, D_MODEL), F32),
                     jax.ShapeDtypeStruct((N_TOK, D_MODEL), BF16)]
    else:
        out_specs = [
            pl.BlockSpec((tm, D_MODEL), lambda i, j: (jnp.minimum(i, npt - 1), 0)),
            pl.BlockSpec((tm, D_MODEL), lambda i, j: (jnp.maximum(i - npt, 0), 0)),
        ]
        out_shape = [jax.ShapeDtypeStruct((N_P, D_MODEL), F32),
                     jax.ShapeDtypeStruct((N_S, D_MODEL), F32)]
    return pl.pallas_call(
        functools.partial(_ffn_kernel, emit_next=nxt is not None, n_p_tiles=npt),
        grid=(N_TOK // tm, nj),
        in_specs=in_specs,
        out_specs=out_specs,
        out_shape=out_shape,
        scratch_shapes=[pltpu.VMEM((tm, D_MODEL), F32)],
        compiler_params=_cparams(("arbitrary", "arbitrary"), 56),
        name="ffn",
    )(*args)


def kernel(x_prompt, x_sample, cache_k, cache_v, state_C, state_n, state_m, c, c_ctx, w_mod, b_mod, norm_pre1, norm_post1, norm_pre2, norm_post2, w_in, q_norm, k_norm, mlstm_gate_bias, mlstm_norm, conv_w, w_branch, w_out, w_ffn_in, w_ffn_out):
    cond = jnp.concatenate([c_ctx[None], c, jnp.zeros((N_COND - 1 - DEC_BATCH, D_MODEL), F32)], axis=0)
    b_mod3 = b_mod.reshape(DEPTH, 1, MOD_W)
    split_mod = lambda m: m.reshape(N_COND, 6, 1, D_MODEL).transpose(1, 0, 2, 3)
    mod = [split_mod(_modulation(cond, w_mod, b_mod3, 0))]
    vec = lambda a: a.reshape(1, -1)
    cos, sin = _rope_tables()

    x = (x_prompt.reshape(N_P, D_MODEL), x_sample.reshape(N_S, D_MODEL))
    h = _prenorm(*x, vec(norm_pre1[0]), mod[0][1], mod[0][0])

    ck = cache_k.reshape(DEC_BATCH, DEPTH, PAST_LEN, KV_WIDTH)
    cv = cache_v.reshape(DEC_BATCH, DEPTH, PAST_LEN, KV_WIDTH)
    st_n = state_n.reshape(DEC_BATCH, DEPTH, 2, M_HEADS, 1, M_DK)
    w_in_t = jnp.swapaxes(w_in, 1, 2)

    cache, state = None, None
    for l in range(DEPTH):
        sh1, sc1, g1, sh2, sc2, g2 = (mod[l][i] for i in range(6))
        gate_bias = jnp.pad(mlstm_gate_bias[l], (0, LANES - GATE_COLS)).reshape(1, LANES)
        mnorm_rep = jnp.broadcast_to(mlstm_norm[l][:, None], (M_WIDTH, LANES))

        if l + 1 < DEPTH:
            *proj, mod_next = _projection(h, w_in_t, w_ffn_in, w_ffn_out, l,
                                          next_mod=(cond, w_mod, b_mod3))
            mod.append(split_mod(mod_next))
        else:
            proj = _projection(h, w_in_t, w_ffn_in, w_ffn_out, l)
        p32, p16, pt16, w_ffn_in16, w_ffn_out16 = proj
        ld, rows, w_branch16, w_out16 = _gates(h, w_in_t, l, gate_bias, w_branch, w_out)

        qw, kw = vec(q_norm[l]), vec(k_norm[l])
        att_p, new_k, new_v = _attention_ctx(p32, qw, kw, l, cache)
        cache = (new_k, new_v)
        att_s = _attention_dec(p32, ck, cv, qw, kw, cos, sin, l)

        ht_p, *state = _mlstm(p16, pt16, ld, rows, n_seq=BATCH, seq_len=SEQ, row0=0, layer=l,
                              prev_state=state, emit_state=True)
        m0 = jnp.broadcast_to(state_m[:, l].reshape(DEC_BATCH, 2, M_HEADS, 1, 1),
                              (DEC_BATCH, 2, M_HEADS, 1, LANES))
        (ht_s,) = _mlstm(p16, pt16, ld, rows, n_seq=DEC_BATCH, seq_len=DEC_SEQ, row0=N_P, layer=l,
                         init=(state_C, st_n, m0))

        merged = _merge(att_p, att_s, ht_p, ht_s, p16, pt16,
                        w_branch16.reshape(N_BRANCH, BRANCH_W, D_MODEL), mnorm_rep,
                        jnp.pad(conv_w[l], ((0, 5), (0, 0))))
        x, h2 = _outproj(merged, w_out16, x, vec(norm_post1[l]), g1,
                         vec(norm_pre2[l]), sc2, sh2)
        if l + 1 < DEPTH:
            nxt = (vec(norm_pre1[l + 1]), mod[l + 1][1], mod[l + 1][0])
            x, h = _ffn(h2, w_ffn_in16, w_ffn_out16, x, vec(norm_post2[l]), g2, nxt)
        else:
            y_p, y_s = _ffn(h2, w_ffn_in16, w_ffn_out16, x, vec(norm_post2[l]), g2)

    new_k, new_v = cache
    c_fin, n_fin, m_fin = state
    return (y_p.reshape(BATCH, SEQ, D_MODEL), y_s.reshape(DEC_BATCH, DEC_SEQ, D_MODEL),
            new_k.reshape(BATCH, DEPTH, SEQ, ATT_KV_HEADS, HEAD_DIM),
            new_v.reshape(BATCH, DEPTH, SEQ, ATT_KV_HEADS, HEAD_DIM),
            c_fin, n_fin.reshape(BATCH, DEPTH, 2, M_HEADS, M_DK), m_fin[:, :, :, :, 0, 0])
```
